```python
import math
import jax, jax.numpy as jnp
from jax import lax
import numpy as np

D_MODEL = 1024
BATCH = 8
SEQ = 4096
DEPTH = 2

CHUNK = 64
Q_BLOCK = 128
ROPE_THETA = 10000.0
LN_EPS = 1e-5
RMS_EPS = 1e-6
NEG_INF = -1e30
DEEPNORM_ALPHA = (2.0 * DEPTH) ** 0.25
DEEPNORM_BETA = (8.0 * DEPTH) ** -0.25

MLA_HEADS = 8
MLA_Q_RANK = 384
MLA_KV_RANK = 256
MLA_NOPE = 64
MLA_ROPE = 32
MLA_V = 64

REL_HEADS = 4
REL_DIM = 64
REL_BACK_CHUNKS = 8
MAX_REL_DIST = 256
REL_TABLE = (CHUNK - 1) + MAX_REL_DIST + 1

SWA_Q_HEADS = 4
SWA_KV_HEADS = 2
SWA_GROUP = SWA_Q_HEADS // SWA_KV_HEADS
SWA_DIM = 64
SWA_WINDOW = 128
SWA_BACK_CHUNKS = SWA_WINDOW // CHUNK

MIX_WIDTH = MLA_HEADS * MLA_V + REL_HEADS * REL_DIM + SWA_Q_HEADS * SWA_DIM

IN_SPLITS = [
    MLA_Q_RANK, MLA_KV_RANK, MLA_ROPE,
    REL_HEADS * REL_DIM, REL_HEADS * REL_DIM, REL_HEADS * REL_DIM,
    SWA_Q_HEADS * SWA_DIM, SWA_KV_HEADS * SWA_DIM, SWA_KV_HEADS * SWA_DIM,
]
IN_COLS = int(sum(IN_SPLITS))
IN_SPLIT_POINTS = [int(c) for c in np.cumsum(IN_SPLITS)[:-1]]

PEER_HEADS = 8
N_KEYS = 128
N_EXPERTS = N_KEYS * N_KEYS
PEER_KEY_DIM = 128
PEER_TOPK = 16
PEER_TOKEN_BLOCK = 128

kernel_name = 'hymba_mla_chunkrel_swasink_peer_deepnorm'


def layer_norm(x, g, b):
    xf = x.astype(jnp.float32)
    mu = jnp.mean(xf, axis=-1, keepdims=True)
    var = jnp.mean(jnp.square(xf - mu), axis=-1, keepdims=True)
    y = (xf - mu) * lax.rsqrt(var + LN_EPS) * g.astype(jnp.float32) + b.astype(jnp.float32)
    return y.astype(x.dtype)


def rms_norm(x, g):
    xf = x.astype(jnp.float32)
    y = xf * lax.rsqrt(jnp.mean(jnp.square(xf), axis=-1, keepdims=True) + RMS_EPS) * g.astype(jnp.float32)
    return y.astype(x.dtype)


def rope_tables(seq, dim):
    inv = ROPE_THETA ** (-jnp.arange(0, dim, 2, dtype=jnp.float32) / dim)
    ang = jnp.arange(seq, dtype=jnp.float32)[:, None] * inv[None, :]
    return jnp.cos(ang), jnp.sin(ang)


def apply_rope(x, cos, sin):
    xf = x.astype(jnp.float32)
    half = xf.shape[-1] // 2
    x1, x2 = xf[..., :half], xf[..., half:]
    c, s = cos[:, None, :], sin[:, None, :]
    return jnp.concatenate([x1 * c - x2 * s, x1 * s + x2 * c], axis=-1).astype(x.dtype)


def chunk_band(x, n_back):
    b, s, h, d = x.shape
    nc = s // CHUNK
    nb = n_back + 1
    xc = jnp.pad(x.reshape(b, nc, CHUNK, h, d), ((0, 0), (n_back, 0), (0, 0), (0, 0), (0, 0)))
    idx = jnp.arange(nc)[:, None] + jnp.arange(nb)[None, :]
    band = xc[:, idx].reshape(b, nc, nb * CHUNK, h, d)
    valid = (jnp.arange(nc)[:, None] - n_back + jnp.arange(nb)[None, :]) >= 0
    return band, jnp.repeat(valid, CHUNK, axis=1)


def mla_attention(c_q, c_kv, k_rope, q_norm, w_uq, kv_norm, w_ukv):
    b, s, _ = c_q.shape
    dk = MLA_NOPE + MLA_ROPE
    q = (rms_norm(c_q, q_norm) @ w_uq).reshape(b, s, MLA_HEADS, dk)
    kv = (rms_norm(c_kv, kv_norm) @ w_ukv).reshape(b, s, MLA_HEADS, MLA_NOPE + MLA_V)
    k_nope, v = kv[..., :MLA_NOPE], kv[..., MLA_NOPE:]
    cos, sin = rope_tables(s, MLA_ROPE)
    q = jnp.concatenate([q[..., :MLA_NOPE], apply_rope(q[..., MLA_NOPE:], cos, sin)], axis=-1)
    k_pe = apply_rope(k_rope[:, :, None, :], cos, sin)
    k = jnp.concatenate([k_nope, jnp.broadcast_to(k_pe, (b, s, MLA_HEADS, MLA_ROPE))], axis=-1)
    scale = dk ** -0.5
    nqb = s // Q_BLOCK
    qb = q.reshape(b, nqb, Q_BLOCK, MLA_HEADS, dk).transpose(1, 0, 2, 3, 4)
    k_chunk = jnp.arange(s) // CHUNK

    def block(args):
        i, qi = args
        sc = jnp.einsum('bqhd,bkhd->bhqk', qi, k).astype(jnp.float32) * scale
        q_chunk = (i * Q_BLOCK + jnp.arange(Q_BLOCK)) // CHUNK
        mask = k_chunk[None, :] <= q_chunk[:, None]
        p = jax.nn.softmax(jnp.where(mask, sc, NEG_INF), axis=-1).astype(v.dtype)
        return jnp.einsum('bhqk,bkhd->bqhd', p, v)

    o = lax.map(block, (jnp.arange(nqb), qb))
    return o.transpose(1, 0, 2, 3, 4).reshape(b, s, MLA_HEADS * MLA_V)


def chunk_relpos_attention(q, k, v, rel_bias):
    b, s, _ = q.shape
    nc = s // CHUNK
    band_len = (REL_BACK_CHUNKS + 1) * CHUNK
    qc = q.reshape(b, nc, CHUNK, REL_HEADS, REL_DIM)
    kb, valid = chunk_band(k.reshape(b, s, REL_HEADS, REL_DIM), REL_BACK_CHUNKS)
    vb, _ = chunk_band(v.reshape(b, s, REL_HEADS, REL_DIM), REL_BACK_CHUNKS)
    sc = jnp.einsum('bcqhd,bcjhd->bhcqj', qc, kb).astype(jnp.float32) * (REL_DIM ** -0.5)
    rel = (REL_BACK_CHUNKS * CHUNK + jnp.arange(CHUNK))[:, None] - jnp.arange(band_len)[None, :]
    rel_idx = jnp.clip(rel, -(CHUNK - 1), MAX_REL_DIST) + (CHUNK - 1)
    bias = rel_bias[:, rel_idx].astype(jnp.float32)
    sc = jnp.where(valid[None, None, :, None, :], sc + bias[None, :, None], NEG_INF)
    p = jax.nn.softmax(sc, axis=-1).astype(v.dtype)
    return jnp.einsum('bhcqj,bcjhd->bcqhd', p, vb).reshape(b, s, REL_HEADS * REL_DIM)


def sliding_sink_attention(q, k, v, sinks):
    b, s, _ = q.shape
    nc = s // CHUNK
    cos, sin = rope_tables(s, SWA_DIM)
    q = apply_rope(q.reshape(b, s, SWA_Q_HEADS, SWA_DIM), cos, sin)
    k = apply_rope(k.reshape(b, s, SWA_KV_HEADS, SWA_DIM), cos, sin)
    qc = q.reshape(b, nc, CHUNK, SWA_KV_HEADS, SWA_GROUP, SWA_DIM)
    kb, valid = chunk_band(k, SWA_BACK_CHUNKS)
    vb, _ = chunk_band(v.reshape(b, s, SWA_KV_HEADS, SWA_DIM), SWA_BACK_CHUNKS)
    sc = jnp.einsum('bcqkgd,bcjkd->bkgcqj', qc, kb).astype(jnp.float32) * (SWA_DIM ** -0.5)
    sc = jnp.where(valid[None, None, None, :, None, :], sc, NEG_INF)
    sink = jnp.broadcast_to(sinks.reshape(SWA_KV_HEADS, SWA_GROUP).astype(jnp.float32)[None, :, :, None, None, None],
                            sc.shape[:-1] + (1,))
    p = jax.nn.softmax(jnp.concatenate([sc, sink], axis=-1), axis=-1)[..., :-1].astype(v.dtype)
    return jnp.einsum('bkgcqj,bcjkd->bcqkgd', p, vb).reshape(b, s, SWA_Q_HEADS * SWA_DIM)


def hybrid_mixer(x, w_in, mla_q_norm, mla_w_uq, mla_kv_norm, mla_w_ukv, rel_bias, swa_sinks, w_out):
    h = x @ w_in
    a_q, a_kv, a_kr, b_q, b_k, b_v, c_q, c_k, c_v = jnp.split(h, IN_SPLIT_POINTS, axis=-1)
    y_a = mla_attention(a_q, a_kv, a_kr, mla_q_norm, mla_w_uq, mla_kv_norm, mla_w_ukv)
    y_b = chunk_relpos_attention(b_q, b_k, b_v, rel_bias)
    y_c = sliding_sink_attention(c_q, c_k, c_v, swa_sinks)
    return jnp.concatenate([y_a, y_b, y_c], axis=-1) @ w_out


def peer_ffn(x, wq, keys, u, v):
    b, s, d = x.shape
    t = b * s
    xf = x.reshape(t, d)
    q = (xf @ wq).reshape(t, PEER_HEADS, 2, PEER_KEY_DIM // 2)
    sc = jnp.einsum('thpd,hpnd->thpn', q, keys).astype(jnp.float32)
    s_top, i_top = lax.top_k(sc, PEER_TOPK)
    cand = (s_top[:, :, 0, :, None] + s_top[:, :, 1, None, :]).reshape(t, PEER_HEADS, PEER_TOPK * PEER_TOPK)
    cand_idx = (i_top[:, :, 0, :, None] * N_KEYS + i_top[:, :, 1, None, :]).reshape(t, PEER_HEADS, PEER_TOPK * PEER_TOPK)
    g_s, pos = lax.top_k(cand, PEER_TOPK)
    experts = jnp.take_along_axis(cand_idx, pos, axis=-1)
    gates = jax.nn.softmax(g_s, axis=-1)
    nb = t // PEER_TOKEN_BLOCK

    def block(args):
        xb, eb, gb = args
        hid = jnp.einsum('td,thkd->thk', xb, u[eb]).astype(jnp.float32)
        act = (jax.nn.gelu(hid, approximate=False) * gb).astype(x.dtype)
        return jnp.einsum('thk,thkd->td', act, v[eb])

    out = lax.map(block, (xf.reshape(nb, PEER_TOKEN_BLOCK, d),
                          experts.reshape(nb, PEER_TOKEN_BLOCK, PEER_HEADS, PEER_TOPK),
                          gates.reshape(nb, PEER_TOKEN_BLOCK, PEER_HEADS, PEER_TOPK)))
    return out.reshape(b, s, d)


def setup_inputs(seed: int = 0) -> dict:
    key = jax.random.key(seed)
    ks = jax.random.split(key, 17)
    n = lambda k, shape: jax.random.normal(k, shape, dtype=jnp.float32)
    dk = MLA_NOPE + MLA_ROPE
    return {
        'x': n(ks[0], (BATCH, SEQ, D_MODEL)),
        'w_in': n(ks[1], (DEPTH, D_MODEL, IN_COLS)) * D_MODEL ** -0.5,
        'mla_q_norm': 1.0 + 0.01 * n(ks[2], (DEPTH, MLA_Q_RANK)),
        'mla_w_uq': n(ks[3], (DEPTH, MLA_Q_RANK, MLA_HEADS * dk)) * MLA_Q_RANK ** -0.5,
        'mla_kv_norm': 1.0 + 0.01 * n(ks[4], (DEPTH, MLA_KV_RANK)),
        'mla_w_ukv': n(ks[5], (DEPTH, MLA_KV_RANK, MLA_HEADS * (MLA_NOPE + MLA_V))) * MLA_KV_RANK ** -0.5,
        'rel_bias': 0.2 * n(ks[6], (DEPTH, REL_HEADS, REL_TABLE)),
        'swa_sinks': 0.5 * n(ks[7], (DEPTH, SWA_Q_HEADS)),
        'w_out': n(ks[8], (DEPTH, MIX_WIDTH, D_MODEL)) * (MIX_WIDTH ** -0.5 * DEEPNORM_BETA),
        'ln1_g': 1.0 + 0.01 * n(ks[9], (DEPTH, D_MODEL)),
        'ln1_b': 0.01 * n(ks[10], (DEPTH, D_MODEL)),
        'peer_wq': n(ks[11], (DEPTH, D_MODEL, PEER_HEADS * PEER_KEY_DIM)) * D_MODEL ** -0.5,
        'peer_keys': n(ks[12], (DEPTH, PEER_HEADS, 2, N_KEYS, PEER_KEY_DIM // 2)) * (PEER_KEY_DIM // 2) ** -0.5,
        'peer_u': n(ks[13], (DEPTH, N_EXPERTS, D_MODEL)) * D_MODEL ** -0.5,
        'peer_v': n(ks[14], (DEPTH, N_EXPERTS, D_MODEL)) * DEEPNORM_BETA,
        'ln2_g': 1.0 + 0.01 * n(ks[15], (DEPTH, D_MODEL)),
        'ln2_b': 0.01 * n(ks[16], (DEPTH, D_MODEL)),
    }


def reference(x, w_in, mla_q_norm, mla_w_uq, mla_kv_norm, mla_w_ukv, rel_bias, swa_sinks, w_out,
              ln1_g, ln1_b, peer_wq, peer_keys, peer_u, peer_v, ln2_g, ln2_b):
    for l in range(DEPTH):
        mix = hybrid_mixer(x, w_in[l], mla_q_norm[l], mla_w_uq[l], mla_kv_norm[l], mla_w_ukv[l],
                           rel_bias[l], swa_sinks[l], w_out[l])
        x = layer_norm(DEEPNORM_ALPHA * x + mix, ln1_g[l], ln1_b[l])
        ffn = peer_ffn(x, peer_wq[l], peer_keys[l], peer_u[l], peer_v[l])
        x = layer_norm(DEEPNORM_ALPHA * x + ffn, ln2_g[l], ln2_b[l])
    return x
```

```python
import functools

import numpy as np
import jax
import jax.numpy as jnp
from jax import lax
from jax.experimental import pallas as pl
from jax.experimental.pallas import tpu as pltpu

F32 = jnp.float32
BF16 = jnp.bfloat16

D_MODEL = 1024
DEPTH = 2
CHUNK = 64
ROPE_THETA = 10000.0
LN_EPS = 1e-5
RMS_EPS = 1e-6
NEG_INF = -1e30
DEEPNORM_ALPHA = (2.0 * DEPTH) ** 0.25

MLA_HEADS = 8
MLA_Q_RANK = 384
MLA_KV_RANK = 256
MLA_NOPE = 64
MLA_ROPE = 32
MLA_V = 64
MLA_DK = MLA_NOPE + MLA_ROPE

REL_HEADS = 4
REL_DIM = 64
REL_BACK_CHUNKS = 8
MAX_REL_DIST = 256

SWA_Q_HEADS = 4
SWA_KV_HEADS = 2
SWA_DIM = 64
SWA_BACK_CHUNKS = 2

IN_SPLITS = [MLA_Q_RANK, MLA_KV_RANK, MLA_ROPE, 256, 256, 256, 256, 128, 128]
IN_SPLIT_POINTS = [int(c) for c in np.cumsum(IN_SPLITS)[:-1]]

PEER_HEADS = 8
N_KEYS = 128
N_EXPERTS = N_KEYS * N_KEYS
PEER_HALF = 64
PEER_TOPK = 16

LANES = 128
HEAD_PAD = 128
SWA_HEAD_ORDER = (0, 2, 1, 3)

C_AQ, C_AKV, C_KR, C_KRR = 0, 384, 640, 768
C_BQ, C_BK, C_BV = 896, 1152, 1408
C_CQ, C_CQR, C_CK, C_CKR, C_CV = 1664, 1920, 2176, 2304, 2432
IN_EXT_COLS = 2560

PROJ_TILE = 512
ATT_BLOCK = 256
ROUTE_TILE = 256
PEER_TOKEN_TILE = 512
PEER_EXPERT_BLOCK = 1024
VMEM_LIMIT = 56 * 1024 * 1024

NT_DIMS = (((1,), (1,)), ((), ()))


def _rot_half_cols(w, head_dim):
    k, c = w.shape
    w3 = w.reshape(k, c // head_dim, head_dim)
    half = head_dim // 2
    return jnp.concatenate([-w3[..., half:], w3[..., :half]], axis=-1).reshape(k, c)


def _place(w, lo, width=LANES):
    return jnp.pad(w, ((0, 0), (lo, width - lo - w.shape[1])))


def _rope_tables(seq, dim):
    inv = ROPE_THETA ** (-jnp.arange(0, dim, 2, dtype=F32) / dim)
    ang = jnp.arange(seq, dtype=F32)[:, None] * inv[None, :]
    return jnp.cos(ang), jnp.sin(ang)


def _make_tables(seq):
    c16, s16 = _rope_tables(seq, MLA_ROPE)
    c32, s32 = _rope_tables(seq, SWA_DIM)
    one = jnp.ones((seq, MLA_NOPE), F32)
    z64 = jnp.zeros((seq, MLA_NOPE), F32)
    z32 = jnp.zeros((seq, LANES - MLA_DK), F32)
    scale_a = MLA_DK ** -0.5
    cos_a = jnp.concatenate([one, c16, c16, z32], axis=1) * scale_a
    sin_a = jnp.concatenate([z64, s16, s16, z32], axis=1) * scale_a
    cos_k = jnp.concatenate([z64, c16, c16, z32], axis=1)
    sin_k = jnp.concatenate([z64, s16, s16, z32], axis=1)
    cos_c = jnp.concatenate([c32, c32, c32, c32], axis=1)
    sin_c = jnp.concatenate([s32, s32, s32, s32], axis=1)
    return jnp.stack([cos_a, sin_a, cos_k, sin_k, cos_c, sin_c])


def _prep_layer(w_in, w_uq, w_ukv, rel_bias, w_out, peer_wq, peer_keys, peer_u, peer_v):
    a_q, a_kv, a_kr, b_q, b_k, b_v, c_q, c_k, c_v = jnp.split(w_in, IN_SPLIT_POINTS, axis=1)
    c_q = c_q.reshape(D_MODEL, SWA_Q_HEADS, SWA_DIM)[:, SWA_HEAD_ORDER, :].reshape(D_MODEL, -1)
    att_scale = REL_DIM ** -0.5
    w_in_ext = jnp.concatenate([
        a_q, a_kv, _place(a_kr, MLA_NOPE), _place(_rot_half_cols(a_kr, MLA_ROPE), MLA_NOPE),
        b_q * att_scale, b_k, b_v,
        c_q * att_scale, _rot_half_cols(c_q, SWA_DIM) * att_scale, c_k, _rot_half_cols(c_k, SWA_DIM), c_v,
    ], axis=1).astype(BF16)

    uq = w_uq.reshape(MLA_Q_RANK, MLA_HEADS, MLA_DK)
    uq_pad = jnp.pad(uq, ((0, 0), (0, 0), (0, HEAD_PAD - MLA_DK))).reshape(MLA_Q_RANK, -1)
    uq_rope = uq[..., MLA_NOPE:]
    half = MLA_ROPE // 2
    uq_rot = jnp.concatenate([-uq_rope[..., half:], uq_rope[..., :half]], axis=-1)
    uq_rot = jnp.pad(uq_rot, ((0, 0), (0, 0), (MLA_NOPE, HEAD_PAD - MLA_DK))).reshape(MLA_Q_RANK, -1)
    w_uq_ext = jnp.concatenate([uq_pad, uq_rot], axis=1).astype(BF16)

    ukv = w_ukv.reshape(MLA_KV_RANK, MLA_HEADS, MLA_NOPE + MLA_V)
    w_uk = jnp.pad(ukv[..., :MLA_NOPE], ((0, 0), (0, 0), (0, HEAD_PAD - MLA_NOPE))).reshape(MLA_KV_RANK, -1).astype(BF16)
    w_uv = ukv[..., MLA_NOPE:].reshape(MLA_KV_RANK, -1).astype(BF16)

    qi = jnp.arange(ATT_BLOCK)[:, None]
    kj = jnp.arange(3 * ATT_BLOCK)[None, :]
    rel = qi + REL_BACK_CHUNKS * CHUNK - kj
    rel_idx = jnp.clip(rel, -(CHUNK - 1), MAX_REL_DIST) + (CHUNK - 1)
    qchunk = qi // CHUNK + REL_BACK_CHUNKS
    kchunk = kj // CHUNK
    visible = (kchunk >= qchunk - REL_BACK_CHUNKS) & (kchunk <= qchunk)
    bias = jnp.where(visible[None], rel_bias[:, rel_idx].astype(F32), NEG_INF)

    n_a = MLA_HEADS * MLA_V
    n_b = REL_HEADS * REL_DIM
    w_out_c = w_out[n_a + n_b:].reshape(SWA_Q_HEADS, SWA_DIM, D_MODEL)[SWA_HEAD_ORDER, :, :].reshape(-1, D_MODEL)
    w_out_ext = jnp.concatenate([w_out[:n_a + n_b], w_out_c], axis=0).astype(BF16)

    keys = peer_keys.reshape(PEER_HEADS * 2, N_KEYS, PEER_HALF)
    keys_lo = jnp.pad(keys, ((0, 0), (0, 0), (0, PEER_HALF)))
    keys_hi = jnp.pad(keys, ((0, 0), (0, 0), (PEER_HALF, 0)))
    is_hi = (jnp.arange(PEER_HEADS * 2) % 2 == 1)[:, None, None]
    keys_ext = jnp.where(is_hi, keys_hi, keys_lo).astype(BF16)

    return dict(w_in=w_in_ext, w_uq=w_uq_ext, w_uk=w_uk, w_uv=w_uv, bias=bias, w_out=w_out_ext,
                wq=peer_wq.astype(BF16), keys=keys_ext, u=peer_u.astype(BF16), v_t=peer_v.T.astype(BF16))


def _params(*sem):
    return pltpu.CompilerParams(dimension_semantics=sem, vmem_limit_bytes=VMEM_LIMIT)


def _layer_norm(z, g, b):
    mu = jnp.mean(z, axis=-1, keepdims=True)
    zc = z - mu
    var = jnp.mean(zc * zc, axis=-1, keepdims=True)
    return zc * lax.rsqrt(var + LN_EPS) * g + b


def _rms_norm(x, g):
    return x * lax.rsqrt(jnp.mean(x * x, axis=-1, keepdims=True) + RMS_EPS) * g


def _proj_kernel(x_ref, win_ref, qn_ref, wuq_ref, kvn_ref, wuk_ref, wuv_ref, tab_ref,
                 qa_ref, ka_ref, va_ref, qb_ref, kb_ref, vb_ref, qc_ref, kc_ref, vc_ref):
    xb = x_ref[...].astype(BF16)

    def proj(lo, width):
        return jnp.dot(xb, win_ref[:, lo:lo + width], preferred_element_type=F32)

    cos_a, sin_a, cos_k, sin_k, cos_c, sin_c = (tab_ref[i] for i in range(6))

    cqn = _rms_norm(proj(C_AQ, MLA_Q_RANK), qn_ref[...]).astype(BF16)
    rot0 = MLA_HEADS * HEAD_PAD
    for h in range(MLA_HEADS):
        lo = h * HEAD_PAD
        qh = jnp.dot(cqn, wuq_ref[:, lo:lo + HEAD_PAD], preferred_element_type=F32)
        qr = jnp.dot(cqn, wuq_ref[:, rot0 + lo:rot0 + lo + HEAD_PAD], preferred_element_type=F32)
        qa_ref[:, lo:lo + HEAD_PAD] = (qh * cos_a + qr * sin_a).astype(BF16)

    ckvn = _rms_norm(proj(C_AKV, MLA_KV_RANK), kvn_ref[...]).astype(BF16)
    k_pe = proj(C_KR, LANES) * cos_k + proj(C_KRR, LANES) * sin_k
    for h in range(MLA_HEADS):
        lo = h * HEAD_PAD
        kh = jnp.dot(ckvn, wuk_ref[:, lo:lo + HEAD_PAD], preferred_element_type=F32)
        ka_ref[:, lo:lo + HEAD_PAD] = (kh + k_pe).astype(BF16)
    va_ref[...] = jnp.dot(ckvn, wuv_ref[...], preferred_element_type=F32).astype(BF16)

    qb_ref[...] = proj(C_BQ, 256).astype(BF16)
    kb_ref[...] = proj(C_BK, 256).astype(BF16)
    vb_ref[...] = proj(C_BV, 256).astype(BF16)

    for t in range(2):
        lo = t * LANES
        qc_ref[:, lo:lo + LANES] = (proj(C_CQ + lo, LANES) * cos_c + proj(C_CQR + lo, LANES) * sin_c).astype(BF16)
    kc_ref[...] = (proj(C_CK, LANES) * cos_c + proj(C_CKR, LANES) * sin_c).astype(BF16)
    vc_ref[...] = proj(C_CV, LANES).astype(BF16)


def _projections(x2d, lp, qn, kvn, tabs, seq):
    t = x2d.shape[0]
    tile = min(PROJ_TILE, seq)
    n_seq = seq // tile
    full = lambda a: pl.BlockSpec(a.shape, lambda i: (0,) * a.ndim)
    row = lambda w: pl.BlockSpec((tile, w), lambda i: (i, 0))
    widths = (1024, 1024, 512, 256, 256, 256, 256, 128, 128)
    return pl.pallas_call(
        _proj_kernel,
        grid=(t // tile,),
        in_specs=[row(D_MODEL), full(lp['w_in']), full(qn), full(lp['w_uq']), full(kvn), full(lp['w_uk']),
                  full(lp['w_uv']), pl.BlockSpec((6, tile, LANES), lambda i: (0, i % n_seq, 0))],
        out_specs=[row(w) for w in widths],
        out_shape=[jax.ShapeDtypeStruct((t, w), BF16) for w in widths],
        compiler_params=_params("parallel"),
        name="projections",
    )(x2d, lp['w_in'], qn, lp['w_uq'], kvn, lp['w_uk'], lp['w_uv'], tabs)


def _mla_kernel(q_ref, k_ref, v_ref, o_ref):
    i = pl.program_id(2)
    bq = q_ref.shape[0]
    left = lax.broadcasted_iota(jnp.int32, (1, LANES), 1) < MLA_V
    qchunk = lax.broadcasted_iota(jnp.int32, (bq, bq), 0) // CHUNK
    kchunk = lax.broadcasted_iota(jnp.int32, (bq, bq), 1) // CHUNK
    diag_visible = kchunk <= qchunk
    res = None
    for r in range(2):
        q = q_ref[:, r * HEAD_PAD:(r + 1) * HEAD_PAD]

        def step(j, carry, masked):
            m, l, acc = carry
            start = pl.multiple_of(j * bq, bq)
            kb = k_ref[pl.ds(start, bq), r * HEAD_PAD:(r + 1) * HEAD_PAD]
            s = lax.dot_general(q, kb, NT_DIMS, preferred_element_type=F32)
            if masked:
                s = jnp.where(diag_visible, s, NEG_INF)
            m_new = jnp.maximum(m, jnp.max(s, axis=-1, keepdims=True))
            p = jnp.exp(s - m_new)
            a = jnp.exp(m - m_new)
            l = a * l + jnp.sum(p, axis=-1, keepdims=True)
            pv = jnp.dot(p.astype(BF16), v_ref[pl.ds(start, bq), :], preferred_element_type=F32)
            return m_new, l, a * acc + pv

        init = (jnp.full((bq, 1), NEG_INF, F32), jnp.zeros((bq, 1), F32), jnp.zeros((bq, LANES), F32))
        carry = lax.fori_loop(0, i, lambda j, c: step(j, c, False), init)
        _, l, acc = step(i, carry, True)
        o = acc / l
        res = o if r == 0 else jnp.where(left, res, o)
    o_ref[...] = res.astype(BF16)


def _mla_attention(qa, ka, va, batch, seq):
    blk = min(ATT_BLOCK, seq)
    nq = seq // blk
    pairs = MLA_HEADS // 2
    return pl.pallas_call(
        _mla_kernel,
        grid=(batch, pairs, nq),
        in_specs=[pl.BlockSpec((blk, 2 * HEAD_PAD), lambda b, p, i: (b * nq + i, p)),
                  pl.BlockSpec((seq, 2 * HEAD_PAD), lambda b, p, i: (b, p)),
                  pl.BlockSpec((seq, 2 * MLA_V), lambda b, p, i: (b, p))],
        out_specs=pl.BlockSpec((blk, 2 * MLA_V), lambda b, p, i: (b * nq + i, p)),
        out_shape=jax.ShapeDtypeStruct((batch * seq, MLA_HEADS * MLA_V), BF16),
        compiler_params=_params("parallel", "parallel", "arbitrary"),
        name="mla_attention",
    )(qa, ka, va)


def _rel_kernel(q_ref, k0_ref, k1_ref, k2_ref, v0_ref, v1_ref, v2_ref, bias_ref, o_ref):
    i = pl.program_id(2)
    blk = q_ref.shape[0]
    kwin = jnp.concatenate([k0_ref[...], k1_ref[...], k2_ref[...]], axis=0)
    vwin = jnp.concatenate([v0_ref[...], v1_ref[...], v2_ref[...]], axis=0)
    col = lax.broadcasted_iota(jnp.int32, (1, 3 * blk), 1)
    in_seq = col >= (2 - i) * blk
    left = lax.broadcasted_iota(jnp.int32, (1, LANES), 1) < REL_DIM
    q = q_ref[...]
    res = None
    for r in range(2):
        qm = jnp.where(left if r == 0 else jnp.logical_not(left), q, jnp.zeros_like(q))
        s = lax.dot_general(qm, kwin, NT_DIMS, preferred_element_type=F32) + bias_ref[r]
        s = jnp.where(in_seq, s, NEG_INF)
        m = jnp.max(s, axis=-1, keepdims=True)
        p = jnp.exp(s - m)
        l = jnp.sum(p, axis=-1, keepdims=True)
        o = jnp.dot(p.astype(BF16), vwin, preferred_element_type=F32) / l
        res = o if r == 0 else jnp.where(left, res, o)
    o_ref[...] = res.astype(BF16)


def _rel_attention(qb, kb, vb, bias, batch, seq):
    blk = ATT_BLOCK
    nq = seq // blk
    pairs = REL_HEADS // 2
    back = lambda d: (lambda b, p, i: (b * nq + jnp.maximum(i - d, 0), p))
    tile = lambda d: pl.BlockSpec((blk, LANES), back(d))
    return pl.pallas_call(
        _rel_kernel,
        grid=(batch, pairs, nq),
        in_specs=[tile(0), tile(2), tile(1), tile(0), tile(2), tile(1), tile(0),
                  pl.BlockSpec((2, blk, 3 * blk), lambda b, p, i: (p, 0, 0))],
        out_specs=tile(0),
        out_shape=jax.ShapeDtypeStruct((batch * seq, REL_HEADS * REL_DIM), BF16),
        compiler_params=_params("parallel", "parallel", "arbitrary"),
        name="rel_attention",
    )(qb, kb, kb, kb, vb, vb, vb, bias)


def _swa_kernel(sink_ref, q_ref, kp_ref, kc_ref, vp_ref, vc_ref, o_ref):
    tile = pl.program_id(1)
    i = pl.program_id(2)
    blk = q_ref.shape[0]
    back = SWA_BACK_CHUNKS * CHUNK
    kwin = jnp.concatenate([kp_ref[blk - back:, :], kc_ref[...]], axis=0)
    vwin = jnp.concatenate([vp_ref[blk - back:, :], vc_ref[...]], axis=0)
    qchunk = lax.broadcasted_iota(jnp.int32, (blk, blk + back), 0) // CHUNK
    col = lax.broadcasted_iota(jnp.int32, (blk, blk + back), 1)
    kchunk = col // CHUNK
    visible = (kchunk >= qchunk) & (kchunk <= qchunk + SWA_BACK_CHUNKS) & ((col >= back) | (i > 0))
    left = lax.broadcasted_iota(jnp.int32, (1, LANES), 1) < SWA_DIM
    q = q_ref[...]
    res = None
    for r in range(2):
        qm = jnp.where(left if r == 0 else jnp.logical_not(left), q, jnp.zeros_like(q))
        s = lax.dot_general(qm, kwin, NT_DIMS, preferred_element_type=F32)
        s = jnp.where(visible, s, NEG_INF)
        sink = sink_ref[2 * r + tile]
        m = jnp.maximum(jnp.max(s, axis=-1, keepdims=True), sink)
        p = jnp.exp(s - m)
        l = jnp.sum(p, axis=-1, keepdims=True) + jnp.exp(sink - m)
        o = jnp.dot(p.astype(BF16), vwin, preferred_element_type=F32) / l
        res = o if r == 0 else jnp.where(left, res, o)
    o_ref[...] = res.astype(BF16)


def _swa_attention(qc, kc, vc, sinks, batch, seq):
    blk = ATT_BLOCK
    nq = seq // blk
    cur = lambda b, t, i: (b * nq + i, 0)
    prev = lambda b, t, i: (b * nq + jnp.maximum(i - 1, 0), 0)
    return pl.pallas_call(
        _swa_kernel,
        grid=(batch, 2, nq),
        in_specs=[pl.BlockSpec(memory_space=pltpu.SMEM),
                  pl.BlockSpec((blk, LANES), lambda b, t, i: (b * nq + i, t)),
                  pl.BlockSpec((blk, LANES), prev), pl.BlockSpec((blk, LANES), cur),
                  pl.BlockSpec((blk, LANES), prev), pl.BlockSpec((blk, LANES), cur)],
        out_specs=pl.BlockSpec((blk, LANES), lambda b, t, i: (b * nq + i, t)),
        out_shape=jax.ShapeDtypeStruct((batch * seq, SWA_Q_HEADS * SWA_DIM), BF16),
        compiler_params=_params("parallel", "parallel", "arbitrary"),
        name="swa_attention",
    )(sinks, qc, kc, kc, vc, vc)


def _take_top(s, rows, n_rows, on_pick):
    m = jnp.max(s, axis=0, keepdims=True)
    idx = jnp.min(jnp.where(s == m, rows, float(n_rows)), axis=0, keepdims=True)
    hit = rows == idx
    on_pick(m, hit)
    return jnp.where(hit, -jnp.inf, s), hit


def _route_kernel(ya_ref, yb_ref, yc_ref, x_ref, wout_ref, g_ref, b_ref, wq_ref, keys_ref,
                  x1_ref, x1b_ref, rb_ref, e2_ref, nb_ref, e1_ref,
                  q_scr, sc_scr, rank_scr, top_scr, cand_scr, w_scr):
    tt = x_ref.shape[0]
    n_a, n_b = ya_ref.shape[1], yb_ref.shape[1]
    mix = (jnp.dot(ya_ref[...], wout_ref[0:n_a, :], preferred_element_type=F32)
           + jnp.dot(yb_ref[...], wout_ref[n_a:n_a + n_b, :], preferred_element_type=F32)
           + jnp.dot(yc_ref[...], wout_ref[n_a + n_b:, :], preferred_element_type=F32))
    x1 = _layer_norm(DEEPNORM_ALPHA * x_ref[...] + mix, g_ref[...], b_ref[...])
    x1_ref[...] = x1
    x1b = x1.astype(BF16)
    x1b_ref[...] = x1b
    q = jnp.dot(x1b, wq_ref[...], preferred_element_type=F32).astype(BF16)
    for h in range(PEER_HEADS):
        q_scr[h] = q[:, h * LANES:(h + 1) * LANES]

    key_rows = lax.broadcasted_iota(jnp.int32, (N_KEYS, tt), 0).astype(F32)

    def half_body(hp, _):
        s0 = lax.dot_general(keys_ref[hp], q_scr[hp // 2], NT_DIMS, preferred_element_type=F32)
        sc_scr[hp] = s0

        def pick(r, carry):
            s, rank = carry

            def on_pick(m, hit):
                top_scr[hp, pl.ds(r, 1), :] = m

            s, hit = _take_top(s, key_rows, N_KEYS, on_pick)
            return s, jnp.where(hit, r.astype(F32), rank)

        _, rank = lax.fori_loop(0, PEER_TOPK, pick, (s0, jnp.full((N_KEYS, tt), float(PEER_TOPK), F32)))
        rank_scr[hp] = rank
        return 0

    lax.fori_loop(0, 2 * PEER_HEADS, half_body, 0)

    n_cand = PEER_TOPK * PEER_TOPK
    cand_rows = lax.broadcasted_iota(jnp.int32, (n_cand, tt), 0).astype(F32)

    def head_body(h, _):
        t1 = top_scr[2 * h]
        t2 = top_scr[2 * h + 1]
        e1t = jnp.exp(t1 - t1[0:1])
        e2t = jnp.exp(t2 - t2[0:1])
        for a in range(PEER_TOPK):
            cand_scr[a * PEER_TOPK:(a + 1) * PEER_TOPK, :] = t1[a:a + 1] + t2
            w_scr[a * PEER_TOPK:(a + 1) * PEER_TOPK, :] = e1t[a:a + 1] * e2t

        def pick(r, carry):
            c, sel = carry
            c, hit = _take_top(c, cand_rows, n_cand, lambda m, hit: None)
            return c, jnp.where(hit, 1.0, sel)

        _, sel = lax.fori_loop(0, PEER_TOPK, pick, (cand_scr[...], jnp.zeros((n_cand, tt), F32)))
        z = jnp.sum(sel * w_scr[...], axis=0, keepdims=True)

        rank1 = rank_scr[2 * h]
        n_sel = jnp.zeros((N_KEYS, tt), F32)
        for a in range(PEER_TOPK):
            cnt = jnp.sum(sel[a * PEER_TOPK:(a + 1) * PEER_TOPK, :], axis=0, keepdims=True)
            n_sel = jnp.where(rank1 == float(a), cnt, n_sel)
        nb_ref[h] = n_sel
        rb_ref[h] = rank_scr[2 * h + 1]
        e1_ref[h] = jnp.exp(sc_scr[2 * h] - t1[0:1]) / z
        e2_ref[h] = jnp.exp(sc_scr[2 * h + 1] - t2[0:1])
        return 0

    lax.fori_loop(0, PEER_HEADS, head_body, 0)


def _route(ya, yb, yc, x2d, lp, g, b):
    t = x2d.shape[0]
    tt = min(ROUTE_TILE, t)
    full = lambda a: pl.BlockSpec(a.shape, lambda i: (0,) * a.ndim)
    row = lambda w: pl.BlockSpec((tt, w), lambda i: (i, 0))
    route_spec = pl.BlockSpec((PEER_HEADS, N_KEYS, tt), lambda i: (0, 0, i))
    route_shape = jax.ShapeDtypeStruct((PEER_HEADS, N_KEYS, t), F32)
    n_cand = PEER_TOPK * PEER_TOPK
    return pl.pallas_call(
        _route_kernel,
        grid=(t // tt,),
        in_specs=[row(ya.shape[1]), row(yb.shape[1]), row(yc.shape[1]), row(D_MODEL), full(lp['w_out']),
                  full(g), full(b), full(lp['wq']), full(lp['keys'])],
        out_specs=[row(D_MODEL), row(D_MODEL), route_spec, route_spec, route_spec, route_spec],
        out_shape=[jax.ShapeDtypeStruct((t, D_MODEL), F32), jax.ShapeDtypeStruct((t, D_MODEL), BF16),
                   route_shape, route_shape, route_shape, route_shape],
        scratch_shapes=[pltpu.VMEM((PEER_HEADS, tt, LANES), BF16),
                        pltpu.VMEM((2 * PEER_HEADS, N_KEYS, tt), F32),
                        pltpu.VMEM((2 * PEER_HEADS, N_KEYS, tt), F32),
                        pltpu.VMEM((2 * PEER_HEADS, PEER_TOPK, tt), F32),
                        pltpu.VMEM((n_cand, tt), F32),
                        pltpu.VMEM((n_cand, tt), F32)],
        compiler_params=_params("parallel"),
        name="route",
    )(ya, yb, yc, x2d, lp['w_out'], g, b, lp['wq'], lp['keys'])


def _peer_kernel(x1b_ref, u_ref, vt_ref, rb_ref, e2_ref, nb_ref, e1_ref, x1_ref, g_ref, b_ref,
                 o_ref, acc_ref, act_ref):
    ei = pl.program_id(1)
    n_sub = u_ref.shape[0] // N_KEYS

    @pl.when(ei == 0)
    def _():
        acc_ref[...] = jnp.zeros_like(acc_ref)

    hid_t = lax.dot_general(u_ref[...], x1b_ref[...], NT_DIMS, preferred_element_type=F32)
    for ii in range(n_sub):
        i_row = ei * n_sub + ii
        hid = hid_t[ii * N_KEYS:(ii + 1) * N_KEYS, :]
        gate = jnp.zeros_like(hid)
        for h in range(PEER_HEADS):
            n_sel = nb_ref[h, pl.ds(i_row, 1), :]
            e1 = e1_ref[h, pl.ds(i_row, 1), :]
            gate = gate + jnp.where(rb_ref[h] < n_sel, e2_ref[h], 0.0) * e1
        act = 0.5 * hid * (1.0 + lax.erf(hid * np.float32(np.sqrt(0.5)))) * gate
        act_ref[ii * N_KEYS:(ii + 1) * N_KEYS, :] = act.astype(BF16)
    acc_ref[...] += jnp.dot(vt_ref[...], act_ref[...], preferred_element_type=F32)

    @pl.when(ei == pl.num_programs(1) - 1)
    def _():
        ffn = acc_ref[...].T
        o_ref[...] = _layer_norm(DEEPNORM_ALPHA * x1_ref[...] + ffn, g_ref[...], b_ref[...])


def _peer(x1, x1b, rb, e2, nb, e1, lp, g, b):
    t = x1.shape[0]
    tt = min(PEER_TOKEN_TILE, t)
    eb = PEER_EXPERT_BLOCK
    full = lambda a: pl.BlockSpec(a.shape, lambda ti, ei: (0,) * a.ndim)
    route_spec = pl.BlockSpec((PEER_HEADS, N_KEYS, tt), lambda ti, ei: (0, 0, ti))
    return pl.pallas_call(
        _peer_kernel,
        grid=(t // tt, N_EXPERTS // eb),
        in_specs=[pl.BlockSpec((tt, D_MODEL), lambda ti, ei: (ti, 0)),
                  pl.BlockSpec((eb, D_MODEL), lambda ti, ei: (ei, 0)),
                  pl.BlockSpec((D_MODEL, eb), lambda ti, ei: (0, ei)),
                  route_spec, route_spec, route_spec, route_spec,
                  pl.BlockSpec((tt, D_MODEL), lambda ti, ei: (ti, 0)), full(g), full(b)],
        out_specs=pl.BlockSpec((tt, D_MODEL), lambda ti, ei: (ti, 0)),
        out_shape=jax.ShapeDtypeStruct((t, D_MODEL), F32),
        scratch_shapes=[pltpu.VMEM((D_MODEL, tt), F32), pltpu.VMEM((eb, tt), BF16)],
        compiler_params=_params("parallel", "arbitrary"),
        name="peer_experts",
    )(x1b, lp['u'], lp['v_t'], rb, e2, nb, e1, x1, g, b)


def kernel(x, w_in, mla_q_norm, mla_w_uq, mla_kv_norm, mla_w_ukv, rel_bias, swa_sinks, w_out, ln1_g, ln1_b,
           peer_wq, peer_keys, peer_u, peer_v, ln2_g, ln2_b):
    batch, seq, d = x.shape
    assert d == D_MODEL and seq % ATT_BLOCK == 0
    tabs = _make_tables(seq)
    h = x.reshape(batch * seq, d)
    for l in range(w_in.shape[0]):
        lp = _prep_layer(w_in[l], mla_w_uq[l], mla_w_ukv[l], rel_bias[l], w_out[l], peer_wq[l], peer_keys[l],
                         peer_u[l], peer_v[l])
        row = lambda a: a[l].reshape(1, -1)
        qa, ka, va, qb, kb, vb, qc, kc, vc = _projections(h, lp, row(mla_q_norm), row(mla_kv_norm), tabs, seq)
        ya = _mla_attention(qa, ka, va, batch, seq)
        yb = _rel_attention(qb, kb, vb, lp['bias'], batch, seq)
        yc = _swa_attention(qc, kc, vc, swa_sinks[l], batch, seq)
        x1, x1b, rb, e2, nb, e1 = _route(ya, yb, yc, h, lp, row(ln1_g), row(ln1_b))
        h = _peer(x1, x1b, rb, e2, nb, e1, lp, row(ln2_g), row(ln2_b))
    return h.reshape(batch, seq, d)
```

```python
import numpy as np
import jax
import jax.numpy as jnp
from jax import lax
from jax.experimental import pallas as pl
from jax.experimental.pallas import tpu as pltpu

F32 = jnp.float32
BF16 = jnp.bfloat16
U32 = jnp.uint32

D_MODEL = 1024
DEPTH = 2
CHUNK = 64
ROPE_THETA = 10000.0
LN_EPS = 1e-5
RMS_EPS = 1e-6
NEG_INF = -1e30
DEEPNORM_ALPHA = (2.0 * DEPTH) ** 0.25

MLA_HEADS = 8
MLA_Q_RANK = 384
MLA_KV_RANK = 256
MLA_NOPE = 64
MLA_ROPE = 32
MLA_V = 64
MLA_DK = MLA_NOPE + MLA_ROPE

REL_HEADS = 4
REL_DIM = 64
REL_BACK_CHUNKS = 8
MAX_REL_DIST = 256

SWA_Q_HEADS = 4
SWA_KV_HEADS = 2
SWA_DIM = 64
SWA_BACK_CHUNKS = 2

IN_SPLITS = [MLA_Q_RANK, MLA_KV_RANK, MLA_ROPE, 256, 256, 256, 256, 128, 128]
IN_SPLIT_POINTS = [int(c) for c in np.cumsum(IN_SPLITS)[:-1]]

PEER_HEADS = 8
N_KEYS = 128
N_EXPERTS = N_KEYS * N_KEYS
PEER_HALF = 64
PEER_TOPK = 16

LANES = 128
SUBLANES = 8
HEAD_PAD = 128
SWA_HEAD_ORDER = (0, 2, 1, 3)

C_AQ, C_AKV, C_KR, C_KRR = 0, 384, 640, 768
C_BQ, C_BK, C_BV = 896, 1152, 1408
C_CQ, C_CQR, C_CK, C_CKR, C_CV = 1664, 1920, 2176, 2304, 2432

PROJ_TILE = 512
ATT_BLOCK = 256
MLA_BLOCK = 512
ROUTE_TILE = 256
PEER_TOKEN_TILE = 512
PEER_EXPERT_BLOCK = 1024
REL_EXT = 1024
VMEM_LIMIT = 56 * 1024 * 1024

NT_DIMS = (((1,), (1,)), ((), ()))


def _rot_half_cols(w, head_dim):
    k, c = w.shape
    w3 = w.reshape(k, c // head_dim, head_dim)
    half = head_dim // 2
    return jnp.concatenate([-w3[..., half:], w3[..., :half]], axis=-1).reshape(k, c)


def _place(w, lo, width=LANES):
    return jnp.pad(w, ((0, 0), (lo, width - lo - w.shape[1])))


def _rope_tables(seq, dim):
    inv = ROPE_THETA ** (-jnp.arange(0, dim, 2, dtype=F32) / dim)
    ang = jnp.arange(seq, dtype=F32)[:, None] * inv[None, :]
    return jnp.cos(ang), jnp.sin(ang)


def _make_tables(seq):
    c16, s16 = _rope_tables(seq, MLA_ROPE)
    c32, s32 = _rope_tables(seq, SWA_DIM)
    one = jnp.ones((seq, MLA_NOPE), F32)
    z64 = jnp.zeros((seq, MLA_NOPE), F32)
    z32 = jnp.zeros((seq, LANES - MLA_DK), F32)
    scale_a = MLA_DK ** -0.5
    cos_a = jnp.concatenate([one, c16, c16, z32], axis=1) * scale_a
    sin_a = jnp.concatenate([z64, s16, s16, z32], axis=1) * scale_a
    cos_k = jnp.concatenate([z64, c16, c16, z32], axis=1)
    sin_k = jnp.concatenate([z64, s16, s16, z32], axis=1)
    cos_c = jnp.concatenate([c32, c32, c32, c32], axis=1)
    sin_c = jnp.concatenate([s32, s32, s32, s32], axis=1)
    return jnp.stack([cos_a, sin_a, cos_k, sin_k, cos_c, sin_c])


def _prep_layer(w_in, w_uq, w_ukv, rel_bias, w_out, peer_wq, peer_keys, peer_u, peer_v):
    a_q, a_kv, a_kr, b_q, b_k, b_v, c_q, c_k, c_v = jnp.split(w_in, IN_SPLIT_POINTS, axis=1)
    c_q = c_q.reshape(D_MODEL, SWA_Q_HEADS, SWA_DIM)[:, SWA_HEAD_ORDER, :].reshape(D_MODEL, -1)
    att_scale = REL_DIM ** -0.5
    w_in_ext = jnp.concatenate([
        a_q, a_kv, _place(a_kr, MLA_NOPE), _place(_rot_half_cols(a_kr, MLA_ROPE), MLA_NOPE),
        b_q * att_scale, b_k, b_v,
        c_q * att_scale, _rot_half_cols(c_q, SWA_DIM) * att_scale, c_k, _rot_half_cols(c_k, SWA_DIM), c_v,
    ], axis=1).astype(BF16)

    uq = w_uq.reshape(MLA_Q_RANK, MLA_HEADS, MLA_DK)
    uq_pad = jnp.pad(uq, ((0, 0), (0, 0), (0, HEAD_PAD - MLA_DK))).reshape(MLA_Q_RANK, -1)
    uq_rope = uq[..., MLA_NOPE:]
    half = MLA_ROPE // 2
    uq_rot = jnp.concatenate([-uq_rope[..., half:], uq_rope[..., :half]], axis=-1)
    uq_rot = jnp.pad(uq_rot, ((0, 0), (0, 0), (MLA_NOPE, HEAD_PAD - MLA_DK))).reshape(MLA_Q_RANK, -1)
    w_uq_ext = jnp.concatenate([uq_pad, uq_rot], axis=1).astype(BF16)

    ukv = w_ukv.reshape(MLA_KV_RANK, MLA_HEADS, MLA_NOPE + MLA_V)
    w_uk = jnp.pad(ukv[..., :MLA_NOPE], ((0, 0), (0, 0), (0, HEAD_PAD - MLA_NOPE))).reshape(MLA_KV_RANK, -1).astype(BF16)
    w_uv_t = ukv[..., MLA_NOPE:].reshape(MLA_KV_RANK, -1).T.astype(BF16)

    m = jnp.arange(REL_EXT)
    ext_idx = jnp.clip(3 * ATT_BLOCK - m, -(CHUNK - 1), MAX_REL_DIST) + (CHUNK - 1)
    rel_ext = rel_bias[:, ext_idx].astype(F32).reshape(REL_HEADS, 1, REL_EXT)

    n_a = MLA_HEADS * MLA_V
    n_b = REL_HEADS * REL_DIM
    w_out_c = w_out[n_a + n_b:].reshape(SWA_Q_HEADS, SWA_DIM, D_MODEL)[SWA_HEAD_ORDER, :, :].reshape(-1, D_MODEL)
    w_out_ext = jnp.concatenate([w_out[:n_a + n_b], w_out_c], axis=0).astype(BF16)

    keys = peer_keys.reshape(PEER_HEADS * 2, N_KEYS, PEER_HALF)
    keys_lo = jnp.pad(keys, ((0, 0), (0, 0), (0, PEER_HALF)))
    keys_hi = jnp.pad(keys, ((0, 0), (0, 0), (PEER_HALF, 0)))
    is_hi = (jnp.arange(PEER_HEADS * 2) % 2 == 1)[:, None, None]
    keys_ext = jnp.where(is_hi, keys_hi, keys_lo).astype(BF16)

    return dict(w_in=w_in_ext, w_uq=w_uq_ext, w_uk=w_uk, w_uv_t=w_uv_t, rel_ext=rel_ext, w_out=w_out_ext,
                wq=peer_wq.astype(BF16), keys=keys_ext, u=peer_u.astype(BF16), v_t=peer_v.T.astype(BF16))


def _params(*sem):
    return pltpu.CompilerParams(dimension_semantics=sem, vmem_limit_bytes=VMEM_LIMIT)


def _layer_norm(z, g, b):
    mu = jnp.mean(z, axis=-1, keepdims=True)
    zc = z - mu
    var = jnp.mean(zc * zc, axis=-1, keepdims=True)
    return zc * lax.rsqrt(var + LN_EPS) * g + b


def _rms_norm(x, g):
    return x * lax.rsqrt(jnp.mean(x * x, axis=-1, keepdims=True) + RMS_EPS) * g


def _proj_kernel(x_ref, win_ref, qn_ref, wuq_ref, kvn_ref, wuk_ref, wuvt_ref, tab_ref,
                 qa_ref, ka_ref, vat_ref, qb_ref, kb_ref, vb_ref, qc_ref, kc_ref, vc_ref):
    xb = x_ref[...].astype(BF16)

    def proj(lo, width):
        return jnp.dot(xb, win_ref[:, lo:lo + width], preferred_element_type=F32)

    cos_a, sin_a, cos_k, sin_k, cos_c, sin_c = (tab_ref[i] for i in range(6))

    cqn = _rms_norm(proj(C_AQ, MLA_Q_RANK), qn_ref[...]).astype(BF16)
    rot0 = MLA_HEADS * HEAD_PAD
    for h in range(MLA_HEADS):
        lo = h * HEAD_PAD
        qh = jnp.dot(cqn, wuq_ref[:, lo:lo + HEAD_PAD], preferred_element_type=F32)
        qr = jnp.dot(cqn, wuq_ref[:, rot0 + lo:rot0 + lo + HEAD_PAD], preferred_element_type=F32)
        qa_ref[:, lo:lo + HEAD_PAD] = (qh * cos_a + qr * sin_a).astype(BF16)

    ckvn = _rms_norm(proj(C_AKV, MLA_KV_RANK), kvn_ref[...]).astype(BF16)
    k_pe = proj(C_KR, LANES) * cos_k + proj(C_KRR, LANES) * sin_k
    for h in range(MLA_HEADS):
        lo = h * HEAD_PAD
        kh = jnp.dot(ckvn, wuk_ref[:, lo:lo + HEAD_PAD], preferred_element_type=F32)
        ka_ref[:, lo:lo + HEAD_PAD] = (kh + k_pe).astype(BF16)
    for c in range(vat_ref.shape[0]):
        rows = ckvn[c * MLA_BLOCK:(c + 1) * MLA_BLOCK, :]
        vat_ref[c] = lax.dot_general(wuvt_ref[...], rows, NT_DIMS, preferred_element_type=F32).astype(BF16)

    qb_ref[...] = proj(C_BQ, 256).astype(BF16)
    kb_ref[...] = proj(C_BK, 256).astype(BF16)
    vb_ref[...] = proj(C_BV, 256).astype(BF16)

    for t in range(2):
        lo = t * LANES
        qc_ref[:, lo:lo + LANES] = (proj(C_CQ + lo, LANES) * cos_c + proj(C_CQR + lo, LANES) * sin_c).astype(BF16)
    kc_ref[...] = (proj(C_CK, LANES) * cos_c + proj(C_CKR, LANES) * sin_c).astype(BF16)
    vc_ref[...] = proj(C_CV, LANES).astype(BF16)


def _projections(x2d, lp, qn, kvn, tabs, seq):
    t = x2d.shape[0]
    tile = min(PROJ_TILE, seq)
    n_seq = seq // tile
    sub = tile // MLA_BLOCK
    full = lambda a: pl.BlockSpec(a.shape, lambda i: (0,) * a.ndim)
    row = lambda w: pl.BlockSpec((tile, w), lambda i: (i, 0))
    sds = lambda w: jax.ShapeDtypeStruct((t, w), BF16)
    n_v = MLA_HEADS * MLA_V
    vat_spec = pl.BlockSpec((sub, n_v, MLA_BLOCK), lambda i: (i, 0, 0))
    vat_sds = jax.ShapeDtypeStruct((t // MLA_BLOCK, n_v, MLA_BLOCK), BF16)
    widths = (256, 256, 256, 256, 128, 128)
    return pl.pallas_call(
        _proj_kernel,
        grid=(t // tile,),
        in_specs=[row(D_MODEL), full(lp['w_in']), full(qn), full(lp['w_uq']), full(kvn), full(lp['w_uk']),
                  full(lp['w_uv_t']), pl.BlockSpec((6, tile, LANES), lambda i: (0, i % n_seq, 0))],
        out_specs=[row(1024), row(1024), vat_spec] + [row(w) for w in widths],
        out_shape=[sds(1024), sds(1024), vat_sds] + [sds(w) for w in widths],
        compiler_params=_params("parallel"),
        name="projections",
    )(x2d, lp['w_in'], qn, lp['w_uq'], kvn, lp['w_uk'], lp['w_uv_t'], tabs)


def _mla_kernel(q_ref, k_ref, vt_ref, o_ref):
    i = pl.program_id(2)
    blk = q_ref.shape[0]
    kchunk = lax.broadcasted_iota(jnp.int32, (blk, blk), 0) // CHUNK
    qchunk = lax.broadcasted_iota(jnp.int32, (blk, blk), 1) // CHUNK
    diag_visible = kchunk <= qchunk
    qs = [q_ref[:, r * HEAD_PAD:(r + 1) * HEAD_PAD] for r in range(2)]

    def step(j, carry, masked):
        start = pl.multiple_of(j * blk, blk)
        out = []
        for r in range(2):
            m, l, acc = carry[r]
            kb = k_ref[pl.ds(start, blk), r * HEAD_PAD:(r + 1) * HEAD_PAD]
            s = lax.dot_general(kb, qs[r], NT_DIMS, preferred_element_type=F32)
            if masked:
                s = jnp.where(diag_visible, s, NEG_INF)
            m_new = jnp.maximum(m, jnp.max(s, axis=0, keepdims=True))
            p = jnp.exp(s - m_new)
            a = jnp.exp(m - m_new)
            l = a * l + jnp.sum(p, axis=0, keepdims=True)
            pv = jnp.dot(vt_ref[j, r * MLA_V:(r + 1) * MLA_V, :], p.astype(BF16), preferred_element_type=F32)
            out.append((m_new, l, a * acc + pv))
        return tuple(out)

    one = (jnp.full((1, blk), NEG_INF, F32), jnp.zeros((1, blk), F32), jnp.zeros((MLA_V, blk), F32))
    carry = lax.fori_loop(0, i, lambda j, c: step(j, c, False), (one, one))
    (_, l0, acc0), (_, l1, acc1) = step(i, carry, True)
    o_t = jnp.concatenate([acc0 / l0, acc1 / l1], axis=0)
    o_ref[...] = o_t.T.astype(BF16)


def _mla_attention(qa, ka, vat, batch, seq):
    blk = MLA_BLOCK
    nq = seq // blk
    pairs = MLA_HEADS // 2
    return pl.pallas_call(
        _mla_kernel,
        grid=(batch, pairs, nq),
        in_specs=[pl.BlockSpec((blk, 2 * HEAD_PAD), lambda b, p, i: (b * nq + i, p)),
                  pl.BlockSpec((seq, 2 * HEAD_PAD), lambda b, p, i: (b, p)),
                  pl.BlockSpec((nq, 2 * MLA_V, blk), lambda b, p, i: (b, p, 0))],
        out_specs=pl.BlockSpec((blk, 2 * MLA_V), lambda b, p, i: (b * nq + i, p)),
        out_shape=jax.ShapeDtypeStruct((batch * seq, MLA_HEADS * MLA_V), BF16),
        compiler_params=_params("parallel", "parallel", "arbitrary"),
        name="mla_attention",
    )(qa, ka, vat)


def _rel_bias_kernel(ext_ref, o_ref):
    blk = o_ref.shape[1]
    base = jnp.broadcast_to(ext_ref[0], (blk, REL_EXT))
    tile = pltpu.roll(base, REL_EXT - blk, 1, stride=1, stride_axis=0)[:, :3 * blk]
    qchunk = lax.broadcasted_iota(jnp.int32, (blk, 3 * blk), 0) // CHUNK + REL_BACK_CHUNKS
    kchunk = lax.broadcasted_iota(jnp.int32, (blk, 3 * blk), 1) // CHUNK
    visible = (kchunk >= qchunk - REL_BACK_CHUNKS) & (kchunk <= qchunk)
    o_ref[0] = jnp.where(visible, tile, NEG_INF)


def _rel_bias_tile(rel_ext):
    blk = ATT_BLOCK
    return pl.pallas_call(
        _rel_bias_kernel,
        grid=(REL_HEADS,),
        in_specs=[pl.BlockSpec((1, 1, REL_EXT), lambda h: (h, 0, 0))],
        out_specs=pl.BlockSpec((1, blk, 3 * blk), lambda h: (h, 0, 0)),
        out_shape=jax.ShapeDtypeStruct((REL_HEADS, blk, 3 * blk), F32),
        compiler_params=_params("parallel"),
        name="rel_bias_tile",
    )(rel_ext)


def _rel_kernel(q_ref, k0_ref, k1_ref, k2_ref, v0_ref, v1_ref, v2_ref, bias_ref, o_ref):
    i = pl.program_id(2)
    blk = q_ref.shape[0]
    kwin = jnp.concatenate([k0_ref[...], k1_ref[...], k2_ref[...]], axis=0)
    vwin = jnp.concatenate([v0_ref[...], v1_ref[...], v2_ref[...]], axis=0)
    col = lax.broadcasted_iota(jnp.int32, (1, 3 * blk), 1)
    in_seq = col >= (2 - i) * blk
    left = lax.broadcasted_iota(jnp.int32, (1, LANES), 1) < REL_DIM
    q = q_ref[...]
    res = None
    for r in range(2):
        qm = jnp.where(left if r == 0 else jnp.logical_not(left), q, jnp.zeros_like(q))
        s = lax.dot_general(qm, kwin, NT_DIMS, preferred_element_type=F32) + bias_ref[r]
        s = jnp.where(in_seq, s, NEG_INF)
        m = jnp.max(s, axis=-1, keepdims=True)
        p = jnp.exp(s - m)
        l = jnp.sum(p, axis=-1, keepdims=True)
        o = jnp.dot(p.astype(BF16), vwin, preferred_element_type=F32) / l
        res = o if r == 0 else jnp.where(left, res, o)
    o_ref[...] = res.astype(BF16)


def _rel_attention(qb, kb, vb, bias, batch, seq):
    blk = ATT_BLOCK
    nq = seq // blk
    pairs = REL_HEADS // 2
    back = lambda d: (lambda b, p, i: (b * nq + jnp.maximum(i - d, 0), p))
    tile = lambda d: pl.BlockSpec((blk, LANES), back(d))
    return pl.pallas_call(
        _rel_kernel,
        grid=(batch, pairs, nq),
        in_specs=[tile(0), tile(2), tile(1), tile(0), tile(2), tile(1), tile(0),
                  pl.BlockSpec((2, blk, 3 * blk), lambda b, p, i: (p, 0, 0))],
        out_specs=tile(0),
        out_shape=jax.ShapeDtypeStruct((batch * seq, REL_HEADS * REL_DIM), BF16),
        compiler_params=_params("parallel", "parallel", "arbitrary"),
        name="rel_attention",
    )(qb, kb, kb, kb, vb, vb, vb, bias)


def _swa_kernel(sink_ref, q_ref, kp_ref, kc_ref, vp_ref, vc_ref, o_ref):
    tile = pl.program_id(1)
    i = pl.program_id(2)
    blk = q_ref.shape[0]
    back = SWA_BACK_CHUNKS * CHUNK
    kwin = jnp.concatenate([kp_ref[blk - back:, :], kc_ref[...]], axis=0)
    vwin = jnp.concatenate([vp_ref[blk - back:, :], vc_ref[...]], axis=0)
    qchunk = lax.broadcasted_iota(jnp.int32, (blk, blk + back), 0) // CHUNK
    col = lax.broadcasted_iota(jnp.int32, (blk, blk + back), 1)
    kchunk = col // CHUNK
    visible = (kchunk >= qchunk) & (kchunk <= qchunk + SWA_BACK_CHUNKS) & ((col >= back) | (i > 0))
    left = lax.broadcasted_iota(jnp.int32, (1, LANES), 1) < SWA_DIM
    q = q_ref[...]
    res = None
    for r in range(2):
        qm = jnp.where(left if r == 0 else jnp.logical_not(left), q, jnp.zeros_like(q))
        s = lax.dot_general(qm, kwin, NT_DIMS, preferred_element_type=F32)
        s = jnp.where(visible, s, NEG_INF)
        sink = sink_ref[2 * r + tile]
        m = jnp.maximum(jnp.max(s, axis=-1, keepdims=True), sink)
        p = jnp.exp(s - m)
        l = jnp.sum(p, axis=-1, keepdims=True) + jnp.exp(sink - m)
        o = jnp.dot(p.astype(BF16), vwin, preferred_element_type=F32) / l
        res = o if r == 0 else jnp.where(left, res, o)
    o_ref[...] = res.astype(BF16)


def _swa_attention(qc, kc, vc, sinks, batch, seq):
    blk = ATT_BLOCK
    nq = seq // blk
    cur = lambda b, t, i: (b * nq + i, 0)
    prev = lambda b, t, i: (b * nq + jnp.maximum(i - 1, 0), 0)
    return pl.pallas_call(
        _swa_kernel,
        grid=(batch, 2, nq),
        in_specs=[pl.BlockSpec(memory_space=pltpu.SMEM),
                  pl.BlockSpec((blk, LANES), lambda b, t, i: (b * nq + i, t)),
                  pl.BlockSpec((blk, LANES), prev), pl.BlockSpec((blk, LANES), cur),
                  pl.BlockSpec((blk, LANES), prev), pl.BlockSpec((blk, LANES), cur)],
        out_specs=pl.BlockSpec((blk, LANES), lambda b, t, i: (b * nq + i, t)),
        out_shape=jax.ShapeDtypeStruct((batch * seq, SWA_Q_HEADS * SWA_DIM), BF16),
        compiler_params=_params("parallel", "parallel", "arbitrary"),
        name="swa_attention",
    )(sinks, qc, kc, kc, vc, vc)


def _oddeven_merge_sort_pairs(n):
    pairs = []

    def merge(lo, hi, r):
        step = r * 2
        if step < hi - lo:
            merge(lo, hi, step)
            merge(lo + r, hi, step)
            pairs.extend((i, i + r) for i in range(lo + r, hi - r, step))
        else:
            pairs.append((lo, lo + r))

    def sort(lo, hi):
        if hi - lo >= 1:
            mid = lo + (hi - lo) // 2
            sort(lo, mid)
            sort(mid + 1, hi)
            merge(lo, hi, 1)

    sort(0, n - 1)
    return pairs


def _bitonic_merge_pairs(n):
    pairs = []
    s = n // 2
    while s >= 1:
        pairs.extend((i, i + s) for i in range(n) if (i // s) % 2 == 0)
        s //= 2
    return pairs


SORT16 = _oddeven_merge_sort_pairs(PEER_TOPK)
BITONIC16 = _bitonic_merge_pairs(PEER_TOPK)
STAIRCASE = [(a, b) for a in range(1, PEER_TOPK) for b in range(PEER_TOPK) if (a + 1) * (b + 1) <= PEER_TOPK]


def _compare_exchange(vals, pairs):
    vals = list(vals)
    for a, b in pairs:
        hi = jnp.maximum(vals[a], vals[b])
        lo = jnp.minimum(vals[a], vals[b])
        vals[a], vals[b] = hi, lo
    return vals


def _merge_top16(xs, ys):
    return _compare_exchange([jnp.maximum(xs[i], ys[PEER_TOPK - 1 - i]) for i in range(PEER_TOPK)], BITONIC16)


def _sorted_top16(s):
    slabs = _compare_exchange([s[SUBLANES * v:SUBLANES * (v + 1), :] for v in range(N_KEYS // SUBLANES)], SORT16)
    for shift in (4, 2, 1):
        slabs = _merge_top16(slabs, [pltpu.roll(x, shift, 0) for x in slabs])
    return slabs


def _count_prefix(test, vals):
    cands = list(vals[:PEER_TOPK - 1])
    count = None
    step = PEER_TOPK // 2
    while step >= 1:
        c = test(cands[step - 1])
        inc = jnp.where(c, float(step), 0.0)
        count = inc if count is None else count + inc
        cands = [jnp.where(c, cands[k + step], cands[k]) for k in range(step - 1)]
        step //= 2
    return count + jnp.where(test(vals[PEER_TOPK - 1]), 1.0, 0.0)


def _dup_bf16_bits(x):
    bits = pltpu.bitcast(x.astype(BF16).astype(F32), U32)
    return bits | (bits >> 16)


def _rows_from_bits(row_bits, n_rows):
    return pltpu.bitcast(jnp.broadcast_to(row_bits, (n_rows // 2, row_bits.shape[1])), BF16)


def _take_top(s, rows, n_rows, on_pick):
    m = jnp.max(s, axis=0, keepdims=True)
    idx = jnp.min(jnp.where(s == m, rows, float(n_rows)), axis=0, keepdims=True)
    hit = rows == idx
    on_pick(m, hit)
    return jnp.where(hit, -jnp.inf, s), hit


def _route_kernel(ya_ref, yb_ref, yc_ref, x_ref, wout_ref, g_ref, b_ref, wq_ref, keys_ref,
                  x1_ref, x1b_ref, rb_ref, e2_ref, nb_ref, e1_ref,
                  q_scr, sc_scr, rank_scr, top_scr, cand_scr, w_scr):
    tt = x_ref.shape[0]
    n_a, n_b = ya_ref.shape[1], yb_ref.shape[1]
    mix = (jnp.dot(ya_ref[...], wout_ref[0:n_a, :], preferred_element_type=F32)
           + jnp.dot(yb_ref[...], wout_ref[n_a:n_a + n_b, :], preferred_element_type=F32)
           + jnp.dot(yc_ref[...], wout_ref[n_a + n_b:, :], preferred_element_type=F32))
    x1 = _layer_norm(DEEPNORM_ALPHA * x_ref[...] + mix, g_ref[...], b_ref[...])
    x1_ref[...] = x1
    x1b = x1.astype(BF16)
    x1b_ref[...] = x1b
    q = jnp.dot(x1b, wq_ref[...], preferred_element_type=F32).astype(BF16)
    for h in range(PEER_HEADS):
        q_scr[h] = q[:, h * LANES:(h + 1) * LANES]

    def emit(h, rank2, n_sel, e1, e2):
        rb_ref[h] = rank2.astype(BF16)
        e2_ref[h] = e2.astype(BF16)
        nb_ref[h] = _dup_bf16_bits(n_sel)
        e1_ref[h] = _dup_bf16_bits(e1)

    n_slab = N_KEYS // SUBLANES
    expected_rank_mass = float(sum(range(1, PEER_TOPK + 1)))

    def fast_head(h, bad):
        s1 = lax.dot_general(keys_ref[2 * h], q_scr[h], NT_DIMS, preferred_element_type=F32)
        s2 = lax.dot_general(keys_ref[2 * h + 1], q_scr[h], NT_DIMS, preferred_element_type=F32)
        sc_scr[2 * h] = s1
        sc_scr[2 * h + 1] = s2
        t1 = _sorted_top16(s1)
        t2 = _sorted_top16(s2)
        neg = jnp.full_like(t1[0], -jnp.inf)
        rest = [t1[a] + t2[b] for a, b in STAIRCASE]
        rest += [neg] * (3 * PEER_TOPK - len(rest))
        groups = [_compare_exchange(rest[g * PEER_TOPK:(g + 1) * PEER_TOPK], SORT16) for g in range(3)]
        row0 = [t1[0] + t2[b] for b in range(PEER_TOPK)]
        ctop = _merge_top16(_merge_top16(row0, groups[0]), _merge_top16(groups[1], groups[2]))
        tau = ctop[PEER_TOPK - 1]
        z = jnp.exp(ctop[0] - ctop[0])
        for r in range(1, PEER_TOPK):
            z = z + jnp.exp(ctop[r] - ctop[0])
        inv_z = 1.0 / z
        rank_mass = jnp.zeros_like(tau)
        sel_mass = jnp.zeros_like(tau)
        ranks, counts, e1s, e2s = [], [], [], []
        for v in range(n_slab):
            a1 = s1[SUBLANES * v:SUBLANES * (v + 1), :]
            a2 = s2[SUBLANES * v:SUBLANES * (v + 1), :]
            rank2 = _count_prefix(lambda top: top > a2, t2)
            n_sel = _count_prefix(lambda top: a1 + top >= tau, t2)
            rank_mass = rank_mass + (float(PEER_TOPK) - rank2)
            sel_mass = sel_mass + n_sel
            ranks.append(rank2)
            counts.append(n_sel)
            e1s.append(jnp.exp(a1 - t1[0]) * inv_z)
            e2s.append(jnp.exp(a2 - t2[0]))
        emit(h, jnp.concatenate(ranks, axis=0), jnp.concatenate(counts, axis=0),
             jnp.concatenate(e1s, axis=0), jnp.concatenate(e2s, axis=0))
        rank_mass = jnp.sum(rank_mass, axis=0, keepdims=True)
        sel_mass = jnp.sum(sel_mass, axis=0, keepdims=True)
        tied = (rank_mass != expected_rank_mass) | (sel_mass != float(PEER_TOPK))
        return jnp.maximum(bad, jnp.where(tied, 1.0, 0.0))

    bad = lax.fori_loop(0, PEER_HEADS, fast_head, jnp.zeros((1, tt), F32))

    @pl.when(jnp.max(bad) > 0.0)
    def _exact_with_ties():
        key_rows = lax.broadcasted_iota(jnp.int32, (N_KEYS, tt), 0).astype(F32)

        def half_body(hp, _):
            s0 = sc_scr[hp]

            def pick(r, carry):
                s, rank = carry

                def on_pick(m, hit):
                    top_scr[hp, pl.ds(r, 1), :] = m

                s, hit = _take_top(s, key_rows, N_KEYS, on_pick)
                return s, jnp.where(hit, r.astype(F32), rank)

            _, rank = lax.fori_loop(0, PEER_TOPK, pick, (s0, jnp.full((N_KEYS, tt), float(PEER_TOPK), F32)))
            rank_scr[hp] = rank
            return 0

        lax.fori_loop(0, 2 * PEER_HEADS, half_body, 0)

        n_cand = PEER_TOPK * PEER_TOPK
        cand_rows = lax.broadcasted_iota(jnp.int32, (n_cand, tt), 0).astype(F32)

        def head_body(h, _):
            t1 = top_scr[2 * h]
            t2 = top_scr[2 * h + 1]
            e1t = jnp.exp(t1 - t1[0:1])
            e2t = jnp.exp(t2 - t2[0:1])
            for a in range(PEER_TOPK):
                cand_scr[a * PEER_TOPK:(a + 1) * PEER_TOPK, :] = t1[a:a + 1] + t2
                w_scr[a * PEER_TOPK:(a + 1) * PEER_TOPK, :] = e1t[a:a + 1] * e2t

            def pick(r, carry):
                c, sel = carry
                c, hit = _take_top(c, cand_rows, n_cand, lambda m, hit: None)
                return c, jnp.where(hit, 1.0, sel)

            _, sel = lax.fori_loop(0, PEER_TOPK, pick, (cand_scr[...], jnp.zeros((n_cand, tt), F32)))
            z = jnp.sum(sel * w_scr[...], axis=0, keepdims=True)
            rank1 = rank_scr[2 * h]
            n_sel = jnp.zeros((N_KEYS, tt), F32)
            for a in range(PEER_TOPK):
                cnt = jnp.sum(sel[a * PEER_TOPK:(a + 1) * PEER_TOPK, :], axis=0, keepdims=True)
                n_sel = jnp.where(rank1 == float(a), cnt, n_sel)
            emit(h, rank_scr[2 * h + 1], n_sel, jnp.exp(sc_scr[2 * h] - t1[0:1]) / z,
                 jnp.exp(sc_scr[2 * h + 1] - t2[0:1]))
            return 0

        lax.fori_loop(0, PEER_HEADS, head_body, 0)


def _route(ya, yb, yc, x2d, lp, g, b):
    t = x2d.shape[0]
    tt = min(ROUTE_TILE, t)
    full = lambda a: pl.BlockSpec(a.shape, lambda i: (0,) * a.ndim)
    row = lambda w: pl.BlockSpec((tt, w), lambda i: (i, 0))
    route_spec = pl.BlockSpec((PEER_HEADS, N_KEYS, tt), lambda i: (0, 0, i))
    route_shape = lambda dt: jax.ShapeDtypeStruct((PEER_HEADS, N_KEYS, t), dt)
    n_cand = PEER_TOPK * PEER_TOPK
    return pl.pallas_call(
        _route_kernel,
        grid=(t // tt,),
        in_specs=[row(ya.shape[1]), row(yb.shape[1]), row(yc.shape[1]), row(D_MODEL), full(lp['w_out']),
                  full(g), full(b), full(lp['wq']), full(lp['keys'])],
        out_specs=[row(D_MODEL), row(D_MODEL), route_spec, route_spec, route_spec, route_spec],
        out_shape=[jax.ShapeDtypeStruct((t, D_MODEL), F32), jax.ShapeDtypeStruct((t, D_MODEL), BF16),
                   route_shape(BF16), route_shape(BF16), route_shape(U32), route_shape(U32)],
        scratch_shapes=[pltpu.VMEM((PEER_HEADS, tt, LANES), BF16),
                        pltpu.VMEM((2 * PEER_HEADS, N_KEYS, tt), F32),
                        pltpu.VMEM((2 * PEER_HEADS, N_KEYS, tt), F32),
                        pltpu.VMEM((2 * PEER_HEADS, PEER_TOPK, tt), F32),
                        pltpu.VMEM((n_cand, tt), F32),
                        pltpu.VMEM((n_cand, tt), F32)],
        compiler_params=_params("parallel"),
        name="route",
    )(ya, yb, yc, x2d, lp['w_out'], g, b, lp['wq'], lp['keys'])


def _peer_kernel(x1b_ref, u_ref, vt_ref, rb_ref, e2_ref, nb_ref, e1_ref, x1_ref, g_ref, b_ref,
                 o_ref, acc_ref, act_ref):
    ei = pl.program_id(1)
    tt = x1b_ref.shape[0]
    n_sub = u_ref.shape[0] // N_KEYS

    @pl.when(ei == 0)
    def _():
        acc_ref[...] = jnp.zeros_like(acc_ref)

    hid_t = lax.dot_general(u_ref[...], x1b_ref[...], NT_DIMS, preferred_element_type=F32)
    for ii in range(n_sub):
        i_row = ei * n_sub + ii
        gate = jnp.zeros((N_KEYS, tt), BF16)
        for h in range(PEER_HEADS):
            n_sel = _rows_from_bits(nb_ref[h, pl.ds(i_row, 1), :], N_KEYS)
            e1 = _rows_from_bits(e1_ref[h, pl.ds(i_row, 1), :], N_KEYS)
            gate = gate + jnp.where(rb_ref[h] < n_sel, e2_ref[h], jnp.zeros((), BF16)) * e1
        hid = hid_t[ii * N_KEYS:(ii + 1) * N_KEYS, :]
        gelu = 0.5 * hid * (1.0 + lax.erf(hid * np.float32(np.sqrt(0.5))))
        act_ref[ii * N_KEYS:(ii + 1) * N_KEYS, :] = gelu.astype(BF16) * gate
    acc_ref[...] += jnp.dot(vt_ref[...], act_ref[...], preferred_element_type=F32)

    @pl.when(ei == pl.num_programs(1) - 1)
    def _():
        ffn = acc_ref[...].T
        o_ref[...] = _layer_norm(DEEPNORM_ALPHA * x1_ref[...] + ffn, g_ref[...], b_ref[...])


def _peer(x1, x1b, rb, e2, nb, e1, lp, g, b):
    t = x1.shape[0]
    tt = min(PEER_TOKEN_TILE, t)
    eb = PEER_EXPERT_BLOCK
    full = lambda a: pl.BlockSpec(a.shape, lambda ti, ei: (0,) * a.ndim)
    route_spec = pl.BlockSpec((PEER_HEADS, N_KEYS, tt), lambda ti, ei: (0, 0, ti))
    return pl.pallas_call(
        _peer_kernel,
        grid=(t // tt, N_EXPERTS // eb),
        in_specs=[pl.BlockSpec((tt, D_MODEL), lambda ti, ei: (ti, 0)),
                  pl.BlockSpec((eb, D_MODEL), lambda ti, ei: (ei, 0)),
                  pl.BlockSpec((D_MODEL, eb), lambda ti, ei: (0, ei)),
                  route_spec, route_spec, route_spec, route_spec,
                  pl.BlockSpec((tt, D_MODEL), lambda ti, ei: (ti, 0)), full(g), full(b)],
        out_specs=pl.BlockSpec((tt, D_MODEL), lambda ti, ei: (ti, 0)),
        out_shape=jax.ShapeDtypeStruct((t, D_MODEL), F32),
        scratch_shapes=[pltpu.VMEM((D_MODEL, tt), F32), pltpu.VMEM((eb, tt), BF16)],
        compiler_params=_params("parallel", "arbitrary"),
        name="peer_experts",
    )(x1b, lp['u'], lp['v_t'], rb, e2, nb, e1, x1, g, b)


def kernel(x, w_in, mla_q_norm, mla_w_uq, mla_kv_norm, mla_w_ukv, rel_bias, swa_sinks, w_out, ln1_g, ln1_b,
           peer_wq, peer_keys, peer_u, peer_v, ln2_g, ln2_b):
    batch, seq, d = x.shape
    assert d == D_MODEL and seq % PROJ_TILE == 0 and PROJ_TILE % MLA_BLOCK == 0 and MLA_BLOCK % ATT_BLOCK == 0
    tabs = _make_tables(seq)
    h = x.reshape(batch * seq, d)
    for l in range(w_in.shape[0]):
        lp = _prep_layer(w_in[l], mla_w_uq[l], mla_w_ukv[l], rel_bias[l], w_out[l], peer_wq[l], peer_keys[l],
                         peer_u[l], peer_v[l])
        row = lambda a: a[l].reshape(1, -1)
        qa, ka, vat, qb, kb, vb, qc, kc, vc = _projections(h, lp, row(mla_q_norm), row(mla_kv_norm), tabs, seq)
        ya = _mla_attention(qa, ka, vat, batch, seq)
        yb = _rel_attention(qb, kb, vb, _rel_bias_tile(lp['rel_ext']), batch, seq)
        yc = _swa_attention(qc, kc, vc, swa_sinks[l], batch, seq)
        x1, x1b, rb, e2, nb, e1 = _route(ya, yb, yc, h, lp, row(ln1_g), row(ln1_b))
        h = _peer(x1, x1b, rb, e2, nb, e1, lp, row(ln2_g), row(ln2_b))
    return h.reshape(batch, seq, d)
```

```python
import numpy as np
import jax
import jax.numpy as jnp
from jax import lax
from jax.experimental import pallas as pl
from jax.experimental.pallas import tpu as pltpu

F32 = jnp.float32
BF16 = jnp.bfloat16
U32 = jnp.uint32

D_MODEL = 1024
DEPTH = 2
CHUNK = 64
ROPE_THETA = 10000.0
LN_EPS = 1e-5
RMS_EPS = 1e-6
NEG_INF = -1e30
DEEPNORM_ALPHA = (2.0 * DEPTH) ** 0.25

MLA_HEADS = 8
MLA_Q_RANK = 384
MLA_KV_RANK = 256
MLA_NOPE = 64
MLA_ROPE = 32
MLA_V = 64
MLA_DK = MLA_NOPE + MLA_ROPE

REL_HEADS = 4
REL_DIM = 64
REL_BACK_CHUNKS = 8
MAX_REL_DIST = 256

SWA_Q_HEADS = 4
SWA_KV_HEADS = 2
SWA_DIM = 64
SWA_BACK_CHUNKS = 2

IN_SPLITS = [MLA_Q_RANK, MLA_KV_RANK, MLA_ROPE, 256, 256, 256, 256, 128, 128]
IN_SPLIT_POINTS = [int(c) for c in np.cumsum(IN_SPLITS)[:-1]]

PEER_HEADS = 8
N_KEYS = 128
N_EXPERTS = N_KEYS * N_KEYS
PEER_HALF = 64
PEER_TOPK = 16

LANES = 128
SUBLANES = 8
HEAD_PAD = 128
SWA_HEAD_ORDER = (0, 2, 1, 3)

C_AQ, C_AKV, C_KR, C_KRR = 0, 384, 640, 768
C_BQ, C_BK, C_BV = 896, 1152, 1408
C_CQ, C_CQR, C_CK, C_CKR, C_CV = 1664, 1920, 2176, 2304, 2432

PROJ_TILE = 512
ATT_BLOCK = 256
MLA_BLOCK = 512
MLA_SUM_ROWS = 16
ROUTE_TILE = 256
PEER_TOKEN_TILE = 512
PEER_EXPERT_BLOCK = 2048
REL_EXT = 1024
VMEM_LIMIT = 56 * 1024 * 1024

NT_DIMS = (((1,), (1,)), ((), ()))


def _rot_half_cols(w, head_dim):
    k, c = w.shape
    w3 = w.reshape(k, c // head_dim, head_dim)
    half = head_dim // 2
    return jnp.concatenate([-w3[..., half:], w3[..., :half]], axis=-1).reshape(k, c)


def _place(w, lo, width=LANES):
    return jnp.pad(w, ((0, 0), (lo, width - lo - w.shape[1])))


def _rope_tables(seq, dim):
    inv = ROPE_THETA ** (-jnp.arange(0, dim, 2, dtype=F32) / dim)
    ang = jnp.arange(seq, dtype=F32)[:, None] * inv[None, :]
    return jnp.cos(ang), jnp.sin(ang)


def _make_tables(seq):
    c16, s16 = _rope_tables(seq, MLA_ROPE)
    c32, s32 = _rope_tables(seq, SWA_DIM)
    one = jnp.ones((seq, MLA_NOPE), F32)
    z64 = jnp.zeros((seq, MLA_NOPE), F32)
    z32 = jnp.zeros((seq, LANES - MLA_DK), F32)
    scale_a = MLA_DK ** -0.5
    cos_a = jnp.concatenate([one, c16, c16, z32], axis=1) * scale_a
    sin_a = jnp.concatenate([z64, s16, s16, z32], axis=1) * scale_a
    cos_k = jnp.concatenate([z64, c16, c16, z32], axis=1)
    sin_k = jnp.concatenate([z64, s16, s16, z32], axis=1)
    cos_c = jnp.concatenate([c32, c32, c32, c32], axis=1)
    sin_c = jnp.concatenate([s32, s32, s32, s32], axis=1)
    return jnp.stack([cos_a, sin_a, cos_k, sin_k, cos_c, sin_c])


def _prep_layer(w_in, w_uq, w_ukv, rel_bias, w_out, peer_wq, peer_keys, peer_u, peer_v):
    a_q, a_kv, a_kr, b_q, b_k, b_v, c_q, c_k, c_v = jnp.split(w_in, IN_SPLIT_POINTS, axis=1)
    c_q = c_q.reshape(D_MODEL, SWA_Q_HEADS, SWA_DIM)[:, SWA_HEAD_ORDER, :].reshape(D_MODEL, -1)
    att_scale = REL_DIM ** -0.5
    w_in_ext = jnp.concatenate([
        a_q, a_kv, _place(a_kr, MLA_NOPE), _place(_rot_half_cols(a_kr, MLA_ROPE), MLA_NOPE),
        b_q * att_scale, b_k, b_v,
        c_q * att_scale, _rot_half_cols(c_q, SWA_DIM) * att_scale, c_k, _rot_half_cols(c_k, SWA_DIM), c_v,
    ], axis=1).astype(BF16)

    uq = w_uq.reshape(MLA_Q_RANK, MLA_HEADS, MLA_DK)
    uq_pad = jnp.pad(uq, ((0, 0), (0, 0), (0, HEAD_PAD - MLA_DK))).reshape(MLA_Q_RANK, -1)
    uq_rope = uq[..., MLA_NOPE:]
    half = MLA_ROPE // 2
    uq_rot = jnp.concatenate([-uq_rope[..., half:], uq_rope[..., :half]], axis=-1)
    uq_rot = jnp.pad(uq_rot, ((0, 0), (0, 0), (MLA_NOPE, HEAD_PAD - MLA_DK))).reshape(MLA_Q_RANK, -1)
    w_uq_ext = jnp.concatenate([uq_pad, uq_rot], axis=1).astype(BF16)

    ukv = w_ukv.reshape(MLA_KV_RANK, MLA_HEADS, MLA_NOPE + MLA_V)
    w_uk = jnp.pad(ukv[..., :MLA_NOPE], ((0, 0), (0, 0), (0, HEAD_PAD - MLA_NOPE))).reshape(MLA_KV_RANK, -1).astype(BF16)
    w_uv_t = ukv[..., MLA_NOPE:].reshape(MLA_KV_RANK, -1).T.astype(BF16)

    m = jnp.arange(REL_EXT)
    ext_idx = jnp.clip(3 * ATT_BLOCK - m, -(CHUNK - 1), MAX_REL_DIST) + (CHUNK - 1)
    rel_ext = rel_bias[:, ext_idx].astype(F32).reshape(REL_HEADS, 1, REL_EXT)

    n_a = MLA_HEADS * MLA_V
    n_b = REL_HEADS * REL_DIM
    w_out_c = w_out[n_a + n_b:].reshape(SWA_Q_HEADS, SWA_DIM, D_MODEL)[SWA_HEAD_ORDER, :, :].reshape(-1, D_MODEL)
    w_out_ext = jnp.concatenate([w_out[:n_a + n_b], w_out_c], axis=0).astype(BF16)

    keys = peer_keys.reshape(PEER_HEADS * 2, N_KEYS, PEER_HALF)
    keys_lo = jnp.pad(keys, ((0, 0), (0, 0), (0, PEER_HALF)))
    keys_hi = jnp.pad(keys, ((0, 0), (0, 0), (PEER_HALF, 0)))
    is_hi = (jnp.arange(PEER_HEADS * 2) % 2 == 1)[:, None, None]
    keys_ext = jnp.where(is_hi, keys_hi, keys_lo).astype(BF16)

    return dict(w_in=w_in_ext, w_uq=w_uq_ext, w_uk=w_uk, w_uv_t=w_uv_t, rel_ext=rel_ext, w_out=w_out_ext,
                wq=peer_wq.astype(BF16), keys=keys_ext, u=peer_u.astype(BF16), v_t=peer_v.T.astype(BF16))


def _params(*sem):
    return pltpu.CompilerParams(dimension_semantics=sem, vmem_limit_bytes=VMEM_LIMIT)


def _layer_norm(z, g, b):
    mu = jnp.mean(z, axis=-1, keepdims=True)
    zc = z - mu
    var = jnp.mean(zc * zc, axis=-1, keepdims=True)
    return zc * lax.rsqrt(var + LN_EPS) * g + b


def _rms_norm(x, g):
    return x * lax.rsqrt(jnp.mean(x * x, axis=-1, keepdims=True) + RMS_EPS) * g


def _proj_kernel(x_ref, win_ref, qn_ref, wuq_ref, kvn_ref, wuk_ref, wuvt_ref, tab_ref,
                 qa_ref, ka_ref, vat_ref, qb_ref, kb_ref, vb_ref, qc_ref, kc_ref, vc_ref):
    xb = x_ref[...].astype(BF16)

    def proj(lo, width):
        return jnp.dot(xb, win_ref[:, lo:lo + width], preferred_element_type=F32)

    cos_a, sin_a, cos_k, sin_k, cos_c, sin_c = (tab_ref[i] for i in range(6))

    cqn = _rms_norm(proj(C_AQ, MLA_Q_RANK), qn_ref[...]).astype(BF16)
    rot0 = MLA_HEADS * HEAD_PAD
    for h in range(MLA_HEADS):
        lo = h * HEAD_PAD
        qh = jnp.dot(cqn, wuq_ref[:, lo:lo + HEAD_PAD], preferred_element_type=F32)
        qr = jnp.dot(cqn, wuq_ref[:, rot0 + lo:rot0 + lo + HEAD_PAD], preferred_element_type=F32)
        qa_ref[:, lo:lo + HEAD_PAD] = (qh * cos_a + qr * sin_a).astype(BF16)

    ckvn = _rms_norm(proj(C_AKV, MLA_KV_RANK), kvn_ref[...]).astype(BF16)
    k_pe = proj(C_KR, LANES) * cos_k + proj(C_KRR, LANES) * sin_k
    for h in range(MLA_HEADS):
        lo = h * HEAD_PAD
        kh = jnp.dot(ckvn, wuk_ref[:, lo:lo + HEAD_PAD], preferred_element_type=F32)
        ka_ref[:, lo:lo + HEAD_PAD] = (kh + k_pe).astype(BF16)
    for c in range(vat_ref.shape[0]):
        rows = ckvn[c * MLA_BLOCK:(c + 1) * MLA_BLOCK, :]
        vat_ref[c] = lax.dot_general(wuvt_ref[...], rows, NT_DIMS, preferred_element_type=F32).astype(BF16)

    qb_ref[...] = proj(C_BQ, 256).astype(BF16)
    kb_ref[...] = proj(C_BK, 256).astype(BF16)
    vb_ref[...] = proj(C_BV, 256).astype(BF16)

    for t in range(2):
        lo = t * LANES
        qc_ref[:, lo:lo + LANES] = (proj(C_CQ + lo, LANES) * cos_c + proj(C_CQR + lo, LANES) * sin_c).astype(BF16)
    kc_ref[...] = (proj(C_CK, LANES) * cos_c + proj(C_CKR, LANES) * sin_c).astype(BF16)
    vc_ref[...] = proj(C_CV, LANES).astype(BF16)


def _projections(x2d, lp, qn, kvn, tabs, seq):
    t = x2d.shape[0]
    tile = min(PROJ_TILE, seq)
    n_seq = seq // tile
    sub = tile // MLA_BLOCK
    full = lambda a: pl.BlockSpec(a.shape, lambda i: (0,) * a.ndim)
    row = lambda w: pl.BlockSpec((tile, w), lambda i: (i, 0))
    sds = lambda w: jax.ShapeDtypeStruct((t, w), BF16)
    n_v = MLA_HEADS * MLA_V
    vat_spec = pl.BlockSpec((sub, n_v, MLA_BLOCK), lambda i: (i, 0, 0))
    vat_sds = jax.ShapeDtypeStruct((t // MLA_BLOCK, n_v, MLA_BLOCK), BF16)
    widths = (256, 256, 256, 256, 128, 128)
    return pl.pallas_call(
        _proj_kernel,
        grid=(t // tile,),
        in_specs=[row(D_MODEL), full(lp['w_in']), full(qn), full(lp['w_uq']), full(kvn), full(lp['w_uk']),
                  full(lp['w_uv_t']), pl.BlockSpec((6, tile, LANES), lambda i: (0, i % n_seq, 0))],
        out_specs=[row(1024), row(1024), vat_spec] + [row(w) for w in widths],
        out_shape=[sds(1024), sds(1024), vat_sds] + [sds(w) for w in widths],
        compiler_params=_params("parallel"),
        name="projections",
    )(x2d, lp['w_in'], qn, lp['w_uq'], kvn, lp['w_uk'], lp['w_uv_t'], tabs)


def _mla_kernel(q_ref, k_ref, vt_ref, o_ref):
    i = pl.program_id(2)
    blk = q_ref.shape[0]
    kchunk = lax.broadcasted_iota(jnp.int32, (blk, blk), 0) // CHUNK
    qchunk = lax.broadcasted_iota(jnp.int32, (blk, blk), 1) // CHUNK
    diag_visible = kchunk <= qchunk
    qs = [q_ref[:, r * HEAD_PAD:(r + 1) * HEAD_PAD] for r in range(2)]
    ones_rows = (lax.broadcasted_iota(jnp.int32, (MLA_SUM_ROWS, blk), 0) == 0).astype(BF16)

    def step(j, carry, masked):
        start = pl.multiple_of(j * blk, blk)
        out = []
        scores = [lax.dot_general(k_ref[pl.ds(start, blk), r * HEAD_PAD:(r + 1) * HEAD_PAD], qs[r], NT_DIMS,
                                  preferred_element_type=F32) for r in range(2)]
        for r in range(2):
            m, acc = carry[r]
            s = scores[r]
            if masked:
                s = jnp.where(diag_visible, s, NEG_INF)
            m_new = jnp.maximum(m, jnp.max(s, axis=0, keepdims=True))
            p = jnp.exp((s - m_new).astype(BF16))
            a = jnp.exp(m - m_new)
            v_aug = jnp.concatenate([vt_ref[j, r * MLA_V:(r + 1) * MLA_V, :], ones_rows], axis=0)
            out.append((m_new, a * acc + jnp.dot(v_aug, p, preferred_element_type=F32)))
        return tuple(out)

    one = (jnp.full((1, blk), NEG_INF, F32), jnp.zeros((MLA_V + MLA_SUM_ROWS, blk), F32))
    carry = lax.fori_loop(0, i, lambda j, c: step(j, c, False), (one, one))
    (_, acc0), (_, acc1) = step(i, carry, True)
    o_t = jnp.concatenate([acc[:MLA_V] / acc[MLA_V:MLA_V + 1] for acc in (acc0, acc1)], axis=0)
    o_ref[...] = o_t.T.astype(BF16)


def _mla_attention(qa, ka, vat, batch, seq):
    blk = MLA_BLOCK
    nq = seq // blk
    pairs = MLA_HEADS // 2
    return pl.pallas_call(
        _mla_kernel,
        grid=(batch, pairs, nq),
        in_specs=[pl.BlockSpec((blk, 2 * HEAD_PAD), lambda b, p, i: (b * nq + i, p)),
                  pl.BlockSpec((seq, 2 * HEAD_PAD), lambda b, p, i: (b, p)),
                  pl.BlockSpec((nq, 2 * MLA_V, blk), lambda b, p, i: (b, p, 0))],
        out_specs=pl.BlockSpec((blk, 2 * MLA_V), lambda b, p, i: (b * nq + i, p)),
        out_shape=jax.ShapeDtypeStruct((batch * seq, MLA_HEADS * MLA_V), BF16),
        compiler_params=_params("parallel", "parallel", "arbitrary"),
        name="mla_attention",
    )(qa, ka, vat)


def _rel_bias_kernel(ext_ref, o_ref):
    blk = o_ref.shape[1]
    base = jnp.broadcast_to(ext_ref[0], (blk, REL_EXT))
    tile = pltpu.roll(base, REL_EXT - blk, 1, stride=1, stride_axis=0)[:, :3 * blk]
    qchunk = lax.broadcasted_iota(jnp.int32, (blk, 3 * blk), 0) // CHUNK + REL_BACK_CHUNKS
    kchunk = lax.broadcasted_iota(jnp.int32, (blk, 3 * blk), 1) // CHUNK
    visible = (kchunk >= qchunk - REL_BACK_CHUNKS) & (kchunk <= qchunk)
    o_ref[0] = jnp.where(visible, tile, NEG_INF)


def _rel_bias_tile(rel_ext):
    blk = ATT_BLOCK
    return pl.pallas_call(
        _rel_bias_kernel,
        grid=(REL_HEADS,),
        in_specs=[pl.BlockSpec((1, 1, REL_EXT), lambda h: (h, 0, 0))],
        out_specs=pl.BlockSpec((1, blk, 3 * blk), lambda h: (h, 0, 0)),
        out_shape=jax.ShapeDtypeStruct((REL_HEADS, blk, 3 * blk), F32),
        compiler_params=_params("parallel"),
        name="rel_bias_tile",
    )(rel_ext)


def _rel_kernel(q_ref, k0_ref, k1_ref, k2_ref, v0_ref, v1_ref, v2_ref, bias_ref, o_ref):
    i = pl.program_id(2)
    blk = q_ref.shape[0]
    kwin = jnp.concatenate([k0_ref[...], k1_ref[...], k2_ref[...]], axis=0)
    vwin = jnp.concatenate([v0_ref[...], v1_ref[...], v2_ref[...]], axis=0)
    col = lax.broadcasted_iota(jnp.int32, (1, 3 * blk), 1)
    in_seq = col >= (2 - i) * blk
    left = lax.broadcasted_iota(jnp.int32, (1, LANES), 1) < REL_DIM
    q = q_ref[...]
    res = None
    head_q = [jnp.where(left if r == 0 else jnp.logical_not(left), q, jnp.zeros_like(q)) for r in range(2)]
    scores = [lax.dot_general(qm, kwin, NT_DIMS, preferred_element_type=F32) for qm in head_q]
    for r in range(2):
        s = jnp.where(in_seq, scores[r] + bias_ref[r], NEG_INF)
        m = jnp.max(s, axis=-1, keepdims=True)
        p = jnp.exp(s - m)
        l = jnp.sum(p, axis=-1, keepdims=True)
        o = jnp.dot(p.astype(BF16), vwin, preferred_element_type=F32) / l
        res = o if r == 0 else jnp.where(left, res, o)
    o_ref[...] = res.astype(BF16)


def _rel_attention(qb, kb, vb, bias, batch, seq):
    blk = ATT_BLOCK
    nq = seq // blk
    pairs = REL_HEADS // 2
    back = lambda d: (lambda b, p, i: (b * nq + jnp.maximum(i - d, 0), p))
    tile = lambda d: pl.BlockSpec((blk, LANES), back(d))
    return pl.pallas_call(
        _rel_kernel,
        grid=(batch, pairs, nq),
        in_specs=[tile(0), tile(2), tile(1), tile(0), tile(2), tile(1), tile(0),
                  pl.BlockSpec((2, blk, 3 * blk), lambda b, p, i: (p, 0, 0))],
        out_specs=tile(0),
        out_shape=jax.ShapeDtypeStruct((batch * seq, REL_HEADS * REL_DIM), BF16),
        compiler_params=_params("parallel", "parallel", "arbitrary"),
        name="rel_attention",
    )(qb, kb, kb, kb, vb, vb, vb, bias)


def _swa_kernel(sink_ref, q_ref, kp_ref, kc_ref, vp_ref, vc_ref, o_ref):
    tile = pl.program_id(1)
    i = pl.program_id(2)
    blk = q_ref.shape[0]
    back = SWA_BACK_CHUNKS * CHUNK
    kwin = jnp.concatenate([kp_ref[blk - back:, :], kc_ref[...]], axis=0)
    vwin = jnp.concatenate([vp_ref[blk - back:, :], vc_ref[...]], axis=0)
    qchunk = lax.broadcasted_iota(jnp.int32, (blk, blk + back), 0) // CHUNK
    col = lax.broadcasted_iota(jnp.int32, (blk, blk + back), 1)
    kchunk = col // CHUNK
    visible = (kchunk >= qchunk) & (kchunk <= qchunk + SWA_BACK_CHUNKS) & ((col >= back) | (i > 0))
    left = lax.broadcasted_iota(jnp.int32, (1, LANES), 1) < SWA_DIM
    q = q_ref[...]
    res = None
    head_q = [jnp.where(left if r == 0 else jnp.logical_not(left), q, jnp.zeros_like(q)) for r in range(2)]
    scores = [lax.dot_general(qm, kwin, NT_DIMS, preferred_element_type=F32) for qm in head_q]
    for r in range(2):
        s = jnp.where(visible, scores[r], NEG_INF)
        sink = sink_ref[2 * r + tile]
        m = jnp.maximum(jnp.max(s, axis=-1, keepdims=True), sink)
        p = jnp.exp(s - m)
        l = jnp.sum(p, axis=-1, keepdims=True) + jnp.exp(sink - m)
        o = jnp.dot(p.astype(BF16), vwin, preferred_element_type=F32) / l
        res = o if r == 0 else jnp.where(left, res, o)
    o_ref[...] = res.astype(BF16)


def _swa_attention(qc, kc, vc, sinks, batch, seq):
    blk = ATT_BLOCK
    nq = seq // blk
    cur = lambda b, t, i: (b * nq + i, 0)
    prev = lambda b, t, i: (b * nq + jnp.maximum(i - 1, 0), 0)
    return pl.pallas_call(
        _swa_kernel,
        grid=(batch, 2, nq),
        in_specs=[pl.BlockSpec(memory_space=pltpu.SMEM),
                  pl.BlockSpec((blk, LANES), lambda b, t, i: (b * nq + i, t)),
                  pl.BlockSpec((blk, LANES), prev), pl.BlockSpec((blk, LANES), cur),
                  pl.BlockSpec((blk, LANES), prev), pl.BlockSpec((blk, LANES), cur)],
        out_specs=pl.BlockSpec((blk, LANES), lambda b, t, i: (b * nq + i, t)),
        out_shape=jax.ShapeDtypeStruct((batch * seq, SWA_Q_HEADS * SWA_DIM), BF16),
        compiler_params=_params("parallel", "parallel", "arbitrary"),
        name="swa_attention",
    )(sinks, qc, kc, kc, vc, vc)


def _oddeven_merge_sort_pairs(n):
    pairs = []

    def merge(lo, hi, r):
        step = r * 2
        if step < hi - lo:
            merge(lo, hi, step)
            merge(lo + r, hi, step)
            pairs.extend((i, i + r) for i in range(lo + r, hi - r, step))
        else:
            pairs.append((lo, lo + r))

    def sort(lo, hi):
        if hi - lo >= 1:
            mid = lo + (hi - lo) // 2
            sort(lo, mid)
            sort(mid + 1, hi)
            merge(lo, hi, 1)

    sort(0, n - 1)
    return pairs


def _bitonic_merge_pairs(n):
    pairs = []
    s = n // 2
    while s >= 1:
        pairs.extend((i, i + s) for i in range(n) if (i // s) % 2 == 0)
        s //= 2
    return pairs


SORT16 = _oddeven_merge_sort_pairs(PEER_TOPK)
BITONIC16 = _bitonic_merge_pairs(PEER_TOPK)
STAIRCASE = [(a, b) for a in range(1, PEER_TOPK) for b in range(PEER_TOPK) if (a + 1) * (b + 1) <= PEER_TOPK]


def _compare_exchange(vals, pairs):
    vals = list(vals)
    for a, b in pairs:
        hi = jnp.maximum(vals[a], vals[b])
        lo = jnp.minimum(vals[a], vals[b])
        vals[a], vals[b] = hi, lo
    return vals


def _merge_top16(xs, ys):
    return _compare_exchange([jnp.maximum(xs[i], ys[PEER_TOPK - 1 - i]) for i in range(PEER_TOPK)], BITONIC16)


def _sorted_top16(s):
    slabs = _compare_exchange([s[SUBLANES * v:SUBLANES * (v + 1), :] for v in range(N_KEYS // SUBLANES)], SORT16)
    for shift in (4, 2, 1):
        slabs = _merge_top16(slabs, [pltpu.roll(x, shift, 0) for x in slabs])
    return slabs


def _count_prefix(test, vals):
    cands = list(vals[:PEER_TOPK - 1])
    count = None
    step = PEER_TOPK // 2
    while step >= 1:
        c = test(cands[step - 1])
        inc = jnp.where(c, float(step), 0.0)
        count = inc if count is None else count + inc
        cands = [jnp.where(c, cands[k + step], cands[k]) for k in range(step - 1)]
        step //= 2
    return count + jnp.where(test(vals[PEER_TOPK - 1]), 1.0, 0.0)


def _dup_bf16_bits(x):
    bits = pltpu.bitcast(x.astype(BF16).astype(F32), U32)
    return bits | (bits >> 16)


def _rows_from_bits(row_bits, n_rows):
    return pltpu.bitcast(jnp.broadcast_to(row_bits, (n_rows // 2, row_bits.shape[1])), BF16)


def _take_top(s, rows, n_rows, on_pick):
    m = jnp.max(s, axis=0, keepdims=True)
    idx = jnp.min(jnp.where(s == m, rows, float(n_rows)), axis=0, keepdims=True)
    hit = rows == idx
    on_pick(m, hit)
    return jnp.where(hit, -jnp.inf, s), hit


def _route_kernel(ya_ref, yb_ref, yc_ref, x_ref, wout_ref, g_ref, b_ref, wq_ref, keys_ref,
                  x1_ref, x1b_ref, rb_ref, e2_ref, nb_ref, e1_ref,
                  q_scr, sc_scr, rank_scr, top_scr, cand_scr, w_scr):
    tt = x_ref.shape[0]
    n_a, n_b = ya_ref.shape[1], yb_ref.shape[1]
    mix = (jnp.dot(ya_ref[...], wout_ref[0:n_a, :], preferred_element_type=F32)
           + jnp.dot(yb_ref[...], wout_ref[n_a:n_a + n_b, :], preferred_element_type=F32)
           + jnp.dot(yc_ref[...], wout_ref[n_a + n_b:, :], preferred_element_type=F32))
    x1 = _layer_norm(DEEPNORM_ALPHA * x_ref[...] + mix, g_ref[...], b_ref[...])
    x1_ref[...] = x1
    x1b = x1.astype(BF16)
    x1b_ref[...] = x1b
    q = jnp.dot(x1b, wq_ref[...], preferred_element_type=F32).astype(BF16)
    for h in range(PEER_HEADS):
        q_scr[h] = q[:, h * LANES:(h + 1) * LANES]

    def emit(h, rank2, n_sel, e1, e2):
        rb_ref[h] = rank2.astype(BF16)
        e2_ref[h] = e2.astype(BF16)
        nb_ref[h] = _dup_bf16_bits(n_sel)
        e1_ref[h] = _dup_bf16_bits(e1)

    n_slab = N_KEYS // SUBLANES

    def fast_head(h, bad):
        s1 = lax.dot_general(keys_ref[2 * h], q_scr[h], NT_DIMS, preferred_element_type=F32)
        s2 = lax.dot_general(keys_ref[2 * h + 1], q_scr[h], NT_DIMS, preferred_element_type=F32)
        sc_scr[2 * h] = s1
        sc_scr[2 * h + 1] = s2
        t1 = _sorted_top16(s1)
        t2 = _sorted_top16(s2)
        neg = jnp.full_like(t1[0], -jnp.inf)
        rest = [t1[a] + t2[b] for a, b in STAIRCASE]
        rest += [neg] * (3 * PEER_TOPK - len(rest))
        groups = [_compare_exchange(rest[g * PEER_TOPK:(g + 1) * PEER_TOPK], SORT16) for g in range(3)]
        row0 = [t1[0] + t2[b] for b in range(PEER_TOPK)]
        ctop = _merge_top16(_merge_top16(row0, groups[0]), _merge_top16(groups[1], groups[2]))
        tau = ctop[PEER_TOPK - 1]
        z = jnp.exp(ctop[0] - ctop[0])
        for r in range(1, PEER_TOPK):
            z = z + jnp.exp(ctop[r] - ctop[0])
        inv_z = 1.0 / z
        in_top = jnp.zeros_like(tau)
        sel_mass = jnp.zeros_like(tau)
        ranks, counts, e1s, e2s = [], [], [], []
        for v in range(n_slab):
            a1 = s1[SUBLANES * v:SUBLANES * (v + 1), :]
            a2 = s2[SUBLANES * v:SUBLANES * (v + 1), :]
            rank2 = _count_prefix(lambda top: top > a2, t2)
            n_sel = _count_prefix(lambda top: a1 + top >= tau, t2)
            in_top = in_top + jnp.minimum(float(PEER_TOPK) - rank2, 1.0)
            sel_mass = sel_mass + n_sel
            ranks.append(rank2)
            counts.append(n_sel)
            e1s.append(jnp.exp(a1 - t1[0]) * inv_z)
            e2s.append(jnp.exp(a2 - t2[0]))
        emit(h, jnp.concatenate(ranks, axis=0), jnp.concatenate(counts, axis=0),
             jnp.concatenate(e1s, axis=0), jnp.concatenate(e2s, axis=0))
        in_top = jnp.sum(in_top, axis=0, keepdims=True)
        sel_mass = jnp.sum(sel_mass, axis=0, keepdims=True)
        tied = (in_top != float(PEER_TOPK)) | (sel_mass != float(PEER_TOPK))
        return jnp.maximum(bad, jnp.where(tied, 1.0, 0.0))

    bad = lax.fori_loop(0, PEER_HEADS, fast_head, jnp.zeros((1, tt), F32))

    @pl.when(jnp.max(bad) > 0.0)
    def _exact_with_ties():
        key_rows = lax.broadcasted_iota(jnp.int32, (N_KEYS, tt), 0).astype(F32)

        def half_body(hp, _):
            s0 = sc_scr[hp]

            def pick(r, carry):
                s, rank = carry

                def on_pick(m, hit):
                    top_scr[hp, pl.ds(r, 1), :] = m

                s, hit = _take_top(s, key_rows, N_KEYS, on_pick)
                return s, jnp.where(hit, r.astype(F32), rank)

            _, rank = lax.fori_loop(0, PEER_TOPK, pick, (s0, jnp.full((N_KEYS, tt), float(PEER_TOPK), F32)))
            rank_scr[hp] = rank
            return 0

        lax.fori_loop(0, 2 * PEER_HEADS, half_body, 0)

        n_cand = PEER_TOPK * PEER_TOPK
        cand_rows = lax.broadcasted_iota(jnp.int32, (n_cand, tt), 0).astype(F32)

        def head_body(h, _):
            t1 = top_scr[2 * h]
            t2 = top_scr[2 * h + 1]
            e1t = jnp.exp(t1 - t1[0:1])
            e2t = jnp.exp(t2 - t2[0:1])
            for a in range(PEER_TOPK):
                cand_scr[a * PEER_TOPK:(a + 1) * PEER_TOPK, :] = t1[a:a + 1] + t2
                w_scr[a * PEER_TOPK:(a + 1) * PEER_TOPK, :] = e1t[a:a + 1] * e2t

            def pick(r, carry):
                c, sel = carry
                c, hit = _take_top(c, cand_rows, n_cand, lambda m, hit: None)
                return c, jnp.where(hit, 1.0, sel)

            _, sel = lax.fori_loop(0, PEER_TOPK, pick, (cand_scr[...], jnp.zeros((n_cand, tt), F32)))
            z = jnp.sum(sel * w_scr[...], axis=0, keepdims=True)
            rank1 = rank_scr[2 * h]
            n_sel = jnp.zeros((N_KEYS, tt), F32)
            for a in range(PEER_TOPK):
                cnt = jnp.sum(sel[a * PEER_TOPK:(a + 1) * PEER_TOPK, :], axis=0, keepdims=True)
                n_sel = jnp.where(rank1 == float(a), cnt, n_sel)
            emit(h, rank_scr[2 * h + 1], n_sel, jnp.exp(sc_scr[2 * h] - t1[0:1]) / z,
                 jnp.exp(sc_scr[2 * h + 1] - t2[0:1]))
            return 0

        lax.fori_loop(0, PEER_HEADS, head_body, 0)


def _route(ya, yb, yc, x2d, lp, g, b):
    t = x2d.shape[0]
    tt = min(ROUTE_TILE, t)
    full = lambda a: pl.BlockSpec(a.shape, lambda i: (0,) * a.ndim)
    row = lambda w: pl.BlockSpec((tt, w), lambda i: (i, 0))
    route_spec = pl.BlockSpec((PEER_HEADS, N_KEYS, tt), lambda i: (0, 0, i))
    route_shape = lambda dt: jax.ShapeDtypeStruct((PEER_HEADS, N_KEYS, t), dt)
    n_cand = PEER_TOPK * PEER_TOPK
    return pl.pallas_call(
        _route_kernel,
        grid=(t // tt,),
        in_specs=[row(ya.shape[1]), row(yb.shape[1]), row(yc.shape[1]), row(D_MODEL), full(lp['w_out']),
                  full(g), full(b), full(lp['wq']), full(lp['keys'])],
        out_specs=[row(D_MODEL), row(D_MODEL), route_spec, route_spec, route_spec, route_spec],
        out_shape=[jax.ShapeDtypeStruct((t, D_MODEL), F32), jax.ShapeDtypeStruct((t, D_MODEL), BF16),
                   route_shape(BF16), route_shape(BF16), route_shape(U32), route_shape(U32)],
        scratch_shapes=[pltpu.VMEM((PEER_HEADS, tt, LANES), BF16),
                        pltpu.VMEM((2 * PEER_HEADS, N_KEYS, tt), F32),
                        pltpu.VMEM((2 * PEER_HEADS, N_KEYS, tt), F32),
                        pltpu.VMEM((2 * PEER_HEADS, PEER_TOPK, tt), F32),
                        pltpu.VMEM((n_cand, tt), F32),
                        pltpu.VMEM((n_cand, tt), F32)],
        compiler_params=_params("parallel"),
        name="route",
    )(ya, yb, yc, x2d, lp['w_out'], g, b, lp['wq'], lp['keys'])


def _peer_kernel(x1b_ref, u_ref, vt_ref, rb_ref, e2_ref, nb_ref, e1_ref, x1_ref, g_ref, b_ref,
                 o_ref, acc_ref, act_ref):
    ei = pl.program_id(1)
    tt = x1b_ref.shape[0]
    n_sub = u_ref.shape[0] // N_KEYS

    @pl.when(ei == 0)
    def _():
        acc_ref[...] = jnp.zeros_like(acc_ref)

    hid_t = lax.dot_general(u_ref[...], x1b_ref[...], NT_DIMS, preferred_element_type=F32)
    for ii in range(n_sub):
        i_row = ei * n_sub + ii
        gate = jnp.zeros((N_KEYS, tt), BF16)
        for h in range(PEER_HEADS):
            n_sel = _rows_from_bits(nb_ref[h, pl.ds(i_row, 1), :], N_KEYS)
            e1 = _rows_from_bits(e1_ref[h, pl.ds(i_row, 1), :], N_KEYS)
            gate = gate + jnp.where(rb_ref[h] < n_sel, e2_ref[h], jnp.zeros((), BF16)) * e1
        hid = hid_t[ii * N_KEYS:(ii + 1) * N_KEYS, :]
        gelu = 0.5 * hid * (1.0 + lax.erf(hid * np.float32(np.sqrt(0.5))))
        act_ref[ii * N_KEYS:(ii + 1) * N_KEYS, :] = gelu.astype(BF16) * gate
    acc_ref[...] += jnp.dot(vt_ref[...], act_ref[...], preferred_element_type=F32)

    @pl.when(ei == pl.num_programs(1) - 1)
    def _():
        ffn = acc_ref[...].T
        o_ref[...] = _layer_norm(DEEPNORM_ALPHA * x1_ref[...] + ffn, g_ref[...], b_ref[...])


def _peer(x1, x1b, rb, e2, nb, e1, lp, g, b):
    t = x1.shape[0]
    tt = min(PEER_TOKEN_TILE, t)
    eb = PEER_EXPERT_BLOCK
    full = lambda a: pl.BlockSpec(a.shape, lambda ti, ei: (0,) * a.ndim)
    route_spec = pl.BlockSpec((PEER_HEADS, N_KEYS, tt), lambda ti, ei: (0, 0, ti))
    return pl.pallas_call(
        _peer_kernel,
        grid=(t // tt, N_EXPERTS // eb),
        in_specs=[pl.BlockSpec((tt, D_MODEL), lambda ti, ei: (ti, 0)),
                  pl.BlockSpec((eb, D_MODEL), lambda ti, ei: (ei, 0)),
                  pl.BlockSpec((D_MODEL, eb), lambda ti, ei: (0, ei)),
                  route_spec, route_spec, route_spec, route_spec,
                  pl.BlockSpec((tt, D_MODEL), lambda ti, ei: (ti, 0)), full(g), full(b)],
        out_specs=pl.BlockSpec((tt, D_MODEL), lambda ti, ei: (ti, 0)),
        out_shape=jax.ShapeDtypeStruct((t, D_MODEL), F32),
        scratch_shapes=[pltpu.VMEM((D_MODEL, tt), F32), pltpu.VMEM((eb, tt), BF16)],
        compiler_params=_params("parallel", "arbitrary"),
        name="peer_experts",
    )(x1b, lp['u'], lp['v_t'], rb, e2, nb, e1, x1, g, b)


def kernel(x, w_in, mla_q_norm, mla_w_uq, mla_kv_norm, mla_w_ukv, rel_bias, swa_sinks, w_out, ln1_g, ln1_b,
           peer_wq, peer_keys, peer_u, peer_v, ln2_g, ln2_b):
    batch, seq, d = x.shape
    assert d == D_MODEL and seq % PROJ_TILE == 0 and PROJ_TILE % MLA_BLOCK == 0 and MLA_BLOCK % ATT_BLOCK == 0
    tabs = _make_tables(seq)
    h = x.reshape(batch * seq, d)
    for l in range(w_in.shape[0]):
        lp = _prep_layer(w_in[l], mla_w_uq[l], mla_w_ukv[l], rel_bias[l], w_out[l], peer_wq[l], peer_keys[l],
                         peer_u[l], peer_v[l])
        row = lambda a: a[l].reshape(1, -1)
        qa, ka, vat, qb, kb, vb, qc, kc, vc = _projections(h, lp, row(mla_q_norm), row(mla_kv_norm), tabs, seq)
        ya = _mla_attention(qa, ka, vat, batch, seq)
        yb = _rel_attention(qb, kb, vb, _rel_bias_tile(lp['rel_ext']), batch, seq)
        yc = _swa_attention(qc, kc, vc, swa_sinks[l], batch, seq)
        x1, x1b, rb, e2, nb, e1 = _route(ya, yb, yc, h, lp, row(ln1_g), row(ln1_b))
        h = _peer(x1, x1b, rb, e2, nb, e1, lp, row(ln2_g), row(ln2_b))
    return h.reshape(batch, seq, d)
```

```python
import numpy as np
import jax
import jax.numpy as jnp
from jax import lax
from jax.experimental import pallas as pl
from jax.experimental.pallas import tpu as pltpu

F32 = jnp.float32
BF16 = jnp.bfloat16

D_MODEL = 1024
DEPTH = 2
CHUNK = 64
ROPE_THETA = 10000.0
LN_EPS = 1e-5
RMS_EPS = 1e-6
NEG_INF = -1e30
DEEPNORM_ALPHA = (2.0 * DEPTH) ** 0.25

MLA_HEADS = 8
MLA_Q_RANK = 384
MLA_KV_RANK = 256
MLA_NOPE = 64
MLA_ROPE = 32
MLA_V = 64
MLA_DK = MLA_NOPE + MLA_ROPE

REL_HEADS = 4
REL_DIM = 64
REL_BACK_CHUNKS = 8
MAX_REL_DIST = 256

SWA_Q_HEADS = 4
SWA_KV_HEADS = 2
SWA_DIM = 64
SWA_BACK_CHUNKS = 2

IN_SPLITS = [MLA_Q_RANK, MLA_KV_RANK, MLA_ROPE, 256, 256, 256, 256, 128, 128]
IN_SPLIT_POINTS = [int(c) for c in np.cumsum(IN_SPLITS)[:-1]]

PEER_HEADS = 8
N_KEYS = 128
N_EXPERTS = N_KEYS * N_KEYS
PEER_HALF = 64
PEER_TOPK = 16

LANES = 128
SUBLANES = 8
HEAD_PAD = 128
SWA_HEAD_ORDER = (0, 2, 1, 3)

C_AQ, C_AKV, C_KR, C_KRR = 0, 384, 640, 768
C_BQ, C_BK, C_BV = 896, 1152, 1408
C_CQ, C_CQR, C_CK, C_CKR, C_CV = 1664, 1920, 2176, 2304, 2432

PROJ_TILE = 512
ATT_BLOCK = 256
MLA_BLOCK = 512
MLA_Q_SPLIT = 1
MLA_SUM_ROWS = 16
ROUTE_TILE = 512
PEER_TOKEN_TILE = 512
PEER_EXPERT_BLOCK = 2048
REL_EXT = 1024
VMEM_LIMIT = 56 * 1024 * 1024

NT_DIMS = (((1,), (1,)), ((), ()))
RSQRT2 = float(np.sqrt(0.5))


def _rot_half_cols(w, head_dim):
    k, c = w.shape
    w3 = w.reshape(k, c // head_dim, head_dim)
    half = head_dim // 2
    return jnp.concatenate([-w3[..., half:], w3[..., :half]], axis=-1).reshape(k, c)


def _place(w, lo, width=LANES):
    return jnp.pad(w, ((0, 0), (lo, width - lo - w.shape[1])))


def _rope_tables(seq, dim):
    inv = ROPE_THETA ** (-jnp.arange(0, dim, 2, dtype=F32) / dim)
    ang = jnp.arange(seq, dtype=F32)[:, None] * inv[None, :]
    return jnp.cos(ang), jnp.sin(ang)


def _make_tables(seq):
    c16, s16 = _rope_tables(seq, MLA_ROPE)
    c32, s32 = _rope_tables(seq, SWA_DIM)
    one = jnp.ones((seq, MLA_NOPE), F32)
    z64 = jnp.zeros((seq, MLA_NOPE), F32)
    z32 = jnp.zeros((seq, LANES - MLA_DK), F32)
    scale_a = MLA_DK ** -0.5
    cos_a = jnp.concatenate([one, c16, c16, z32], axis=1) * scale_a
    sin_a = jnp.concatenate([z64, s16, s16, z32], axis=1) * scale_a
    cos_k = jnp.concatenate([z64, c16, c16, z32], axis=1)
    sin_k = jnp.concatenate([z64, s16, s16, z32], axis=1)
    cos_c = jnp.concatenate([c32, c32, c32, c32], axis=1)
    sin_c = jnp.concatenate([s32, s32, s32, s32], axis=1)
    return jnp.stack([cos_a, sin_a, cos_k, sin_k, cos_c, sin_c])


def _prep_layer(w_in, w_uq, w_ukv, rel_bias, w_out, peer_wq, peer_keys, peer_u, peer_v):
    a_q, a_kv, a_kr, b_q, b_k, b_v, c_q, c_k, c_v = jnp.split(w_in, IN_SPLIT_POINTS, axis=1)
    c_q = c_q.reshape(D_MODEL, SWA_Q_HEADS, SWA_DIM)[:, SWA_HEAD_ORDER, :].reshape(D_MODEL, -1)
    att_scale = REL_DIM ** -0.5
    w_in_ext = jnp.concatenate([
        a_q, a_kv, _place(a_kr, MLA_NOPE), _place(_rot_half_cols(a_kr, MLA_ROPE), MLA_NOPE),
        b_q * att_scale, b_k, b_v,
        c_q * att_scale, _rot_half_cols(c_q, SWA_DIM) * att_scale, c_k, _rot_half_cols(c_k, SWA_DIM), c_v,
    ], axis=1).astype(BF16)

    uq = w_uq.reshape(MLA_Q_RANK, MLA_HEADS, MLA_DK)
    uq_pad = jnp.pad(uq, ((0, 0), (0, 0), (0, HEAD_PAD - MLA_DK))).reshape(MLA_Q_RANK, -1)
    uq_rope = uq[..., MLA_NOPE:]
    half = MLA_ROPE // 2
    uq_rot = jnp.concatenate([-uq_rope[..., half:], uq_rope[..., :half]], axis=-1)
    uq_rot = jnp.pad(uq_rot, ((0, 0), (0, 0), (MLA_NOPE, HEAD_PAD - MLA_DK))).reshape(MLA_Q_RANK, -1)
    w_uq_ext = jnp.concatenate([uq_pad, uq_rot], axis=1).astype(BF16)

    ukv = w_ukv.reshape(MLA_KV_RANK, MLA_HEADS, MLA_NOPE + MLA_V)
    w_uk = jnp.pad(ukv[..., :MLA_NOPE], ((0, 0), (0, 0), (0, HEAD_PAD - MLA_NOPE))).reshape(MLA_KV_RANK, -1).astype(BF16)
    w_uv_t = ukv[..., MLA_NOPE:].reshape(MLA_KV_RANK, -1).T.astype(BF16)

    m = jnp.arange(REL_EXT)
    ext_idx = jnp.clip(3 * ATT_BLOCK - m, -(CHUNK - 1), MAX_REL_DIST) + (CHUNK - 1)
    rel_ext = rel_bias[:, ext_idx].astype(F32).reshape(REL_HEADS, 1, REL_EXT)

    n_a = MLA_HEADS * MLA_V
    n_b = REL_HEADS * REL_DIM
    w_out_c = w_out[n_a + n_b:].reshape(SWA_Q_HEADS, SWA_DIM, D_MODEL)[SWA_HEAD_ORDER, :, :].reshape(-1, D_MODEL)
    w_out_ext = jnp.concatenate([w_out[:n_a + n_b], w_out_c], axis=0).astype(BF16)

    keys = peer_keys.reshape(PEER_HEADS * 2, N_KEYS, PEER_HALF)
    keys_lo = jnp.pad(keys, ((0, 0), (0, 0), (0, PEER_HALF)))
    keys_hi = jnp.pad(keys, ((0, 0), (0, 0), (PEER_HALF, 0)))
    is_hi = (jnp.arange(PEER_HEADS * 2) % 2 == 1)[:, None, None]
    keys_ext = jnp.where(is_hi, keys_hi, keys_lo).astype(BF16)

    return dict(w_in=w_in_ext, w_uq=w_uq_ext, w_uk=w_uk, w_uv_t=w_uv_t, rel_ext=rel_ext, w_out=w_out_ext,
                wq=peer_wq.astype(BF16), keys=keys_ext, u=peer_u.astype(BF16), v_t=peer_v.T.astype(BF16))


def _params(*sem):
    return pltpu.CompilerParams(dimension_semantics=sem, vmem_limit_bytes=VMEM_LIMIT)


def _layer_norm(z, g, b):
    mu = jnp.mean(z, axis=-1, keepdims=True)
    zc = z - mu
    var = jnp.mean(zc * zc, axis=-1, keepdims=True)
    return zc * lax.rsqrt(var + LN_EPS) * g + b


def _rms_norm(x, g):
    return x * lax.rsqrt(jnp.mean(x * x, axis=-1, keepdims=True) + RMS_EPS) * g


def _proj_kernel(x_ref, win_ref, qn_ref, wuq_ref, kvn_ref, wuk_ref, wuvt_ref, tab_ref,
                 qa_ref, ka_ref, vat_ref, qb_ref, kb_ref, vb_ref, qc_ref, kc_ref, vc_ref):
    xb = x_ref[...].astype(BF16)
    h_all = jnp.dot(xb, win_ref[...], preferred_element_type=F32)

    def proj(lo, width):
        return h_all[:, lo:lo + width]

    cos_a, sin_a, cos_k, sin_k, cos_c, sin_c = (tab_ref[i] for i in range(6))

    cqn = _rms_norm(proj(C_AQ, MLA_Q_RANK), qn_ref[...]).astype(BF16)
    rot0 = MLA_HEADS * HEAD_PAD
    q_all = jnp.dot(cqn, wuq_ref[...], preferred_element_type=F32)
    for h in range(MLA_HEADS):
        lo = h * HEAD_PAD
        qh = q_all[:, lo:lo + HEAD_PAD]
        qr = q_all[:, rot0 + lo:rot0 + lo + HEAD_PAD]
        qa_ref[:, lo:lo + HEAD_PAD] = (qh * cos_a + qr * sin_a).astype(BF16)

    ckvn = _rms_norm(proj(C_AKV, MLA_KV_RANK), kvn_ref[...]).astype(BF16)
    k_pe = proj(C_KR, LANES) * cos_k + proj(C_KRR, LANES) * sin_k
    k_all = jnp.dot(ckvn, wuk_ref[...], preferred_element_type=F32)
    for h in range(MLA_HEADS):
        lo = h * HEAD_PAD
        ka_ref[:, lo:lo + HEAD_PAD] = (k_all[:, lo:lo + HEAD_PAD] + k_pe).astype(BF16)
    for c in range(vat_ref.shape[0]):
        rows = ckvn[c * MLA_BLOCK:(c + 1) * MLA_BLOCK, :]
        vat_ref[c] = lax.dot_general(wuvt_ref[...], rows, NT_DIMS, preferred_element_type=F32).astype(BF16)

    qb_ref[...] = proj(C_BQ, 256).astype(BF16)
    kb_ref[...] = proj(C_BK, 256).astype(BF16)
    vb_ref[...] = proj(C_BV, 256).astype(BF16)

    for t in range(2):
        lo = t * LANES
        qc_ref[:, lo:lo + LANES] = (proj(C_CQ + lo, LANES) * cos_c + proj(C_CQR + lo, LANES) * sin_c).astype(BF16)
    kc_ref[...] = (proj(C_CK, LANES) * cos_c + proj(C_CKR, LANES) * sin_c).astype(BF16)
    vc_ref[...] = proj(C_CV, LANES).astype(BF16)


def _projections(x2d, lp, qn, kvn, tabs, seq):
    t = x2d.shape[0]
    tile = min(PROJ_TILE, seq)
    n_seq = seq // tile
    sub = tile // MLA_BLOCK
    full = lambda a: pl.BlockSpec(a.shape, lambda i: (0,) * a.ndim)
    row = lambda w: pl.BlockSpec((tile, w), lambda i: (i, 0))
    sds = lambda w: jax.ShapeDtypeStruct((t, w), BF16)
    n_v = MLA_HEADS * MLA_V
    vat_spec = pl.BlockSpec((sub, n_v, MLA_BLOCK), lambda i: (i, 0, 0))
    vat_sds = jax.ShapeDtypeStruct((t // MLA_BLOCK, n_v, MLA_BLOCK), BF16)
    widths = (256, 256, 256, 256, 128, 128)
    return pl.pallas_call(
        _proj_kernel,
        grid=(t // tile,),
        in_specs=[row(D_MODEL), full(lp['w_in']), full(qn), full(lp['w_uq']), full(kvn), full(lp['w_uk']),
                  full(lp['w_uv_t']), pl.BlockSpec((6, tile, LANES), lambda i: (0, i % n_seq, 0))],
        out_specs=[row(1024), row(1024), vat_spec] + [row(w) for w in widths],
        out_shape=[sds(1024), sds(1024), vat_sds] + [sds(w) for w in widths],
        compiler_params=_params("parallel"),
        name="projections",
    )(x2d, lp['w_in'], qn, lp['w_uq'], kvn, lp['w_uk'], lp['w_uv_t'], tabs)


def _mla_kernel(q_ref, k_ref, vt_ref, o_ref):
    i = pl.program_id(2)
    blk = q_ref.shape[0]
    qw = blk // MLA_Q_SPLIT
    chains = [(r, c) for r in range(2) for c in range(MLA_Q_SPLIT)]
    kchunk = lax.broadcasted_iota(jnp.int32, (blk, qw), 0) // CHUNK
    qchunk = lax.broadcasted_iota(jnp.int32, (blk, qw), 1) // CHUNK
    diag_visible = [kchunk <= qchunk + c * (qw // CHUNK) for c in range(MLA_Q_SPLIT)]
    qs = {(r, c): q_ref[c * qw:(c + 1) * qw, r * HEAD_PAD:(r + 1) * HEAD_PAD] for r, c in chains}
    ones_rows = (lax.broadcasted_iota(jnp.int32, (MLA_SUM_ROWS, blk), 0) == 0).astype(BF16)

    def step(j, carry, masked):
        start = pl.multiple_of(j * blk, blk)
        out = []
        scores = [lax.dot_general(k_ref[pl.ds(start, blk), r * HEAD_PAD:(r + 1) * HEAD_PAD], qs[r, c], NT_DIMS,
                                  preferred_element_type=F32) for r, c in chains]
        for n, (r, c) in enumerate(chains):
            m, acc = carry[n]
            s = scores[n]
            if masked:
                s = jnp.where(diag_visible[c], s, NEG_INF)
            m_new = jnp.maximum(m, jnp.max(s, axis=0, keepdims=True))
            p = jnp.exp((s - m_new).astype(BF16))
            a = jnp.exp(m - m_new)
            v_aug = jnp.concatenate([vt_ref[j, r * MLA_V:(r + 1) * MLA_V, :], ones_rows], axis=0)
            out.append((m_new, a * acc + jnp.dot(v_aug, p, preferred_element_type=F32)))
        return tuple(out)

    one = (jnp.full((1, qw), NEG_INF, F32), jnp.zeros((MLA_V + MLA_SUM_ROWS, qw), F32))
    carry = lax.fori_loop(0, i, lambda j, c: step(j, c, False), (one,) * len(chains))
    accs = [acc for _, acc in step(i, carry, True)]
    outs = [acc[:MLA_V] / acc[MLA_V:MLA_V + 1] for acc in accs]
    o_t = jnp.concatenate([jnp.concatenate(outs[r * MLA_Q_SPLIT:(r + 1) * MLA_Q_SPLIT], axis=1) for r in range(2)], axis=0)
    o_ref[...] = o_t.T.astype(BF16)


def _mla_attention(qa, ka, vat, batch, seq):
    blk = MLA_BLOCK
    nq = seq // blk
    pairs = MLA_HEADS // 2
    return pl.pallas_call(
        _mla_kernel,
        grid=(batch, pairs, nq),
        in_specs=[pl.BlockSpec((blk, 2 * HEAD_PAD), lambda b, p, i: (b * nq + i, p)),
                  pl.BlockSpec((seq, 2 * HEAD_PAD), lambda b, p, i: (b, p)),
                  pl.BlockSpec((nq, 2 * MLA_V, blk), lambda b, p, i: (b, p, 0))],
        out_specs=pl.BlockSpec((blk, 2 * MLA_V), lambda b, p, i: (b * nq + i, p)),
        out_shape=jax.ShapeDtypeStruct((batch * seq, MLA_HEADS * MLA_V), BF16),
        compiler_params=_params("parallel", "parallel", "arbitrary"),
        name="mla_attention",
    )(qa, ka, vat)


def _rel_bias_kernel(ext_ref, o_ref):
    blk = o_ref.shape[1]
    base = jnp.broadcast_to(ext_ref[0], (blk, REL_EXT))
    tile = pltpu.roll(base, REL_EXT - blk, 1, stride=1, stride_axis=0)[:, :3 * blk]
    qchunk = lax.broadcasted_iota(jnp.int32, (blk, 3 * blk), 0) // CHUNK + REL_BACK_CHUNKS
    kchunk = lax.broadcasted_iota(jnp.int32, (blk, 3 * blk), 1) // CHUNK
    visible = (kchunk >= qchunk - REL_BACK_CHUNKS) & (kchunk <= qchunk)
    o_ref[0] = jnp.where(visible, tile, NEG_INF)


def _rel_bias_tile(rel_ext):
    blk = ATT_BLOCK
    return pl.pallas_call(
        _rel_bias_kernel,
        grid=(REL_HEADS,),
        in_specs=[pl.BlockSpec((1, 1, REL_EXT), lambda h: (h, 0, 0))],
        out_specs=pl.BlockSpec((1, blk, 3 * blk), lambda h: (h, 0, 0)),
        out_shape=jax.ShapeDtypeStruct((REL_HEADS, blk, 3 * blk), F32),
        compiler_params=_params("parallel"),
        name="rel_bias_tile",
    )(rel_ext)


def _rel_kernel(q_ref, k0_ref, k1_ref, k2_ref, v0_ref, v1_ref, v2_ref, bias_ref, o_ref):
    i = pl.program_id(2)
    blk = q_ref.shape[0]
    kwin = jnp.concatenate([k0_ref[...], k1_ref[...], k2_ref[...]], axis=0)
    vwin = jnp.concatenate([v0_ref[...], v1_ref[...], v2_ref[...]], axis=0)
    col = lax.broadcasted_iota(jnp.int32, (1, 3 * blk), 1)
    in_seq = col >= (2 - i) * blk
    left = lax.broadcasted_iota(jnp.int32, (1, LANES), 1) < REL_DIM
    q = q_ref[...]
    res = None
    head_q = [jnp.where(left if r == 0 else jnp.logical_not(left), q, jnp.zeros_like(q)) for r in range(2)]
    scores = [lax.dot_general(qm, kwin, NT_DIMS, preferred_element_type=F32) for qm in head_q]
    for r in range(2):
        s = jnp.where(in_seq, scores[r] + bias_ref[r], NEG_INF)
        m = jnp.max(s, axis=-1, keepdims=True)
        p = jnp.exp(s - m)
        l = jnp.sum(p, axis=-1, keepdims=True)
        o = jnp.dot(p.astype(BF16), vwin, preferred_element_type=F32) / l
        res = o if r == 0 else jnp.where(left, res, o)
    o_ref[...] = res.astype(BF16)


def _rel_attention(qb, kb, vb, bias, batch, seq):
    blk = ATT_BLOCK
    nq = seq // blk
    pairs = REL_HEADS // 2
    back = lambda d: (lambda b, p, i: (b * nq + jnp.maximum(i - d, 0), p))
    tile = lambda d: pl.BlockSpec((blk, LANES), back(d))
    return pl.pallas_call(
        _rel_kernel,
        grid=(batch, pairs, nq),
        in_specs=[tile(0), tile(2), tile(1), tile(0), tile(2), tile(1), tile(0),
                  pl.BlockSpec((2, blk, 3 * blk), lambda b, p, i: (p, 0, 0))],
        out_specs=tile(0),
        out_shape=jax.ShapeDtypeStruct((batch * seq, REL_HEADS * REL_DIM), BF16),
        compiler_params=_params("parallel", "parallel", "arbitrary"),
        name="rel_attention",
    )(qb, kb, kb, kb, vb, vb, vb, bias)


def _swa_kernel(sink_ref, q_ref, kp_ref, kc_ref, vp_ref, vc_ref, o_ref):
    tile = pl.program_id(1)
    i = pl.program_id(2)
    blk = q_ref.shape[0]
    back = SWA_BACK_CHUNKS * CHUNK
    kwin = jnp.concatenate([kp_ref[blk - back:, :], kc_ref[...]], axis=0)
    vwin = jnp.concatenate([vp_ref[blk - back:, :], vc_ref[...]], axis=0)
    qchunk = lax.broadcasted_iota(jnp.int32, (blk, blk + back), 0) // CHUNK
    col = lax.broadcasted_iota(jnp.int32, (blk, blk + back), 1)
    kchunk = col // CHUNK
    visible = (kchunk >= qchunk) & (kchunk <= qchunk + SWA_BACK_CHUNKS) & ((col >= back) | (i > 0))
    left = lax.broadcasted_iota(jnp.int32, (1, LANES), 1) < SWA_DIM
    q = q_ref[...]
    res = None
    head_q = [jnp.where(left if r == 0 else jnp.logical_not(left), q, jnp.zeros_like(q)) for r in range(2)]
    scores = [lax.dot_general(qm, kwin, NT_DIMS, preferred_element_type=F32) for qm in head_q]
    for r in range(2):
        s = jnp.where(visible, scores[r], NEG_INF)
        sink = sink_ref[2 * r + tile]
        m = jnp.maximum(jnp.max(s, axis=-1, keepdims=True), sink)
        p = jnp.exp(s - m)
        l = jnp.sum(p, axis=-1, keepdims=True) + jnp.exp(sink - m)
        o = jnp.dot(p.astype(BF16), vwin, preferred_element_type=F32) / l
        res = o if r == 0 else jnp.where(left, res, o)
    o_ref[...] = res.astype(BF16)


def _swa_attention(qc, kc, vc, sinks, batch, seq):
    blk = ATT_BLOCK
    nq = seq // blk
    cur = lambda b, t, i: (b * nq + i, 0)
    prev = lambda b, t, i: (b * nq + jnp.maximum(i - 1, 0), 0)
    return pl.pallas_call(
        _swa_kernel,
        grid=(batch, 2, nq),
        in_specs=[pl.BlockSpec(memory_space=pltpu.SMEM),
                  pl.BlockSpec((blk, LANES), lambda b, t, i: (b * nq + i, t)),
                  pl.BlockSpec((blk, LANES), prev), pl.BlockSpec((blk, LANES), cur),
                  pl.BlockSpec((blk, LANES), prev), pl.BlockSpec((blk, LANES), cur)],
        out_specs=pl.BlockSpec((blk, LANES), lambda b, t, i: (b * nq + i, t)),
        out_shape=jax.ShapeDtypeStruct((batch * seq, SWA_Q_HEADS * SWA_DIM), BF16),
        compiler_params=_params("parallel", "parallel", "arbitrary"),
        name="swa_attention",
    )(sinks, qc, kc, kc, vc, vc)


def _oddeven_merge_sort_pairs(n):
    pairs = []

    def merge(lo, hi, r):
        step = r * 2
        if step < hi - lo:
            merge(lo, hi, step)
            merge(lo + r, hi, step)
            pairs.extend((i, i + r) for i in range(lo + r, hi - r, step))
        else:
            pairs.append((lo, lo + r))

    def sort(lo, hi):
        if hi - lo >= 1:
            mid = lo + (hi - lo) // 2
            sort(lo, mid)
            sort(mid + 1, hi)
            merge(lo, hi, 1)

    sort(0, n - 1)
    return pairs


def _bitonic_merge_pairs(n):
    pairs = []
    s = n // 2
    while s >= 1:
        pairs.extend((i, i + s) for i in range(n) if (i // s) % 2 == 0)
        s //= 2
    return pairs


SORT16 = _oddeven_merge_sort_pairs(PEER_TOPK)
BITONIC16 = _bitonic_merge_pairs(PEER_TOPK)
STAIRCASE = [(a, b) for a in range(1, PEER_TOPK) for b in range(PEER_TOPK) if (a + 1) * (b + 1) <= PEER_TOPK]


def _compare_exchange(vals, pairs):
    vals = list(vals)
    for a, b in pairs:
        hi = jnp.maximum(vals[a], vals[b])
        lo = jnp.minimum(vals[a], vals[b])
        vals[a], vals[b] = hi, lo
    return vals


def _merge_top16(xs, ys):
    return _compare_exchange([jnp.maximum(xs[i], ys[PEER_TOPK - 1 - i]) for i in range(PEER_TOPK)], BITONIC16)


def _sorted_top16(s):
    slabs = _compare_exchange([s[SUBLANES * v:SUBLANES * (v + 1), :] for v in range(N_KEYS // SUBLANES)], SORT16)
    for shift in (4, 2, 1):
        slabs = _merge_top16(slabs, [pltpu.roll(x, shift, 0) for x in slabs])
    return slabs


def _count_prefix(test, vals):
    cands = list(vals[:PEER_TOPK - 1])
    count = None
    step = PEER_TOPK // 2
    while step >= 1:
        c = test(cands[step - 1])
        inc = jnp.where(c, float(step), 0.0)
        count = inc if count is None else count + inc
        cands = [jnp.where(c, cands[k + step], cands[k]) for k in range(step - 1)]
        step //= 2
    return count + jnp.where(test(vals[PEER_TOPK - 1]), 1.0, 0.0)


def _take_top(s, rows, n_rows, on_pick):
    m = jnp.max(s, axis=0, keepdims=True)
    idx = jnp.min(jnp.where(s == m, rows, float(n_rows)), axis=0, keepdims=True)
    hit = rows == idx
    on_pick(m, hit)
    return jnp.where(hit, -jnp.inf, s), hit


def _route_kernel(ya_ref, yb_ref, yc_ref, x_ref, wout_ref, g_ref, b_ref, wq_ref, keys_ref,
                  x1_ref, x1b_ref, rb_ref, e2_ref, nb_ref, e1_ref,
                  q_scr, sc_scr, rank_scr, top_scr, cand_scr, w_scr):
    tt = x_ref.shape[0]
    n_a, n_b = ya_ref.shape[1], yb_ref.shape[1]
    mix = (jnp.dot(ya_ref[...], wout_ref[0:n_a, :], preferred_element_type=F32)
           + jnp.dot(yb_ref[...], wout_ref[n_a:n_a + n_b, :], preferred_element_type=F32)
           + jnp.dot(yc_ref[...], wout_ref[n_a + n_b:, :], preferred_element_type=F32))
    x1 = _layer_norm(DEEPNORM_ALPHA * x_ref[...] + mix, g_ref[...], b_ref[...])
    x1_ref[...] = x1
    x1b = x1.astype(BF16)
    x1b_ref[...] = x1b
    q = jnp.dot(x1b, wq_ref[...], preferred_element_type=F32).astype(BF16)
    for h in range(PEER_HEADS):
        q_scr[h] = q[:, h * LANES:(h + 1) * LANES]

    def emit(h, rank2, n_sel, e1, e2):
        rb_ref[h] = rank2.astype(BF16)
        e2_ref[h] = e2.astype(BF16)
        nb_ref[h] = n_sel
        e1_ref[h] = e1

    n_slab = N_KEYS // SUBLANES

    def fast_head(h, bad):
        s1 = lax.dot_general(keys_ref[2 * h], q_scr[h], NT_DIMS, preferred_element_type=F32)
        s2 = lax.dot_general(keys_ref[2 * h + 1], q_scr[h], NT_DIMS, preferred_element_type=F32)
        sc_scr[2 * h] = s1
        sc_scr[2 * h + 1] = s2
        t1 = _sorted_top16(s1)
        t2 = _sorted_top16(s2)
        neg = jnp.full_like(t1[0], -jnp.inf)
        rest = [t1[a] + t2[b] for a, b in STAIRCASE]
        rest += [neg] * (3 * PEER_TOPK - len(rest))
        groups = [_compare_exchange(rest[g * PEER_TOPK:(g + 1) * PEER_TOPK], SORT16) for g in range(3)]
        row0 = [t1[0] + t2[b] for b in range(PEER_TOPK)]
        ctop = _merge_top16(_merge_top16(row0, groups[0]), _merge_top16(groups[1], groups[2]))
        tau = ctop[PEER_TOPK - 1]
        z = jnp.exp(ctop[0] - ctop[0])
        for r in range(1, PEER_TOPK):
            z = z + jnp.exp(ctop[r] - ctop[0])
        inv_z = 1.0 / z
        in_top = jnp.zeros_like(tau)
        sel_mass = jnp.zeros_like(tau)
        ranks, counts, e1s, e2s = [], [], [], []
        for v in range(n_slab):
            a1 = s1[SUBLANES * v:SUBLANES * (v + 1), :]
            a2 = s2[SUBLANES * v:SUBLANES * (v + 1), :]
            rank2 = _count_prefix(lambda top: top > a2, t2)
            n_sel = _count_prefix(lambda top: a1 + top >= tau, t2)
            in_top = in_top + jnp.minimum(float(PEER_TOPK) - rank2, 1.0)
            sel_mass = sel_mass + n_sel
            ranks.append(rank2)
            counts.append(n_sel)
            e1s.append(jnp.exp(a1 - t1[0]) * inv_z)
            e2s.append(jnp.exp(a2 - t2[0]))
        emit(h, jnp.concatenate(ranks, axis=0), jnp.concatenate(counts, axis=0),
             jnp.concatenate(e1s, axis=0), jnp.concatenate(e2s, axis=0))
        in_top = jnp.sum(in_top, axis=0, keepdims=True)
        sel_mass = jnp.sum(sel_mass, axis=0, keepdims=True)
        tied = (in_top != float(PEER_TOPK)) | (sel_mass != float(PEER_TOPK))
        return jnp.maximum(bad, jnp.where(tied, 1.0, 0.0))

    bad = lax.fori_loop(0, PEER_HEADS, fast_head, jnp.zeros((1, tt), F32))

    @pl.when(jnp.max(bad) > 0.0)
    def _exact_with_ties():
        key_rows = lax.broadcasted_iota(jnp.int32, (N_KEYS, tt), 0).astype(F32)

        def half_body(hp, _):
            s0 = sc_scr[hp]

            def pick(r, carry):
                s, rank = carry

                def on_pick(m, hit):
                    top_scr[hp, pl.ds(r, 1), :] = m

                s, hit = _take_top(s, key_rows, N_KEYS, on_pick)
                return s, jnp.where(hit, r.astype(F32), rank)

            _, rank = lax.fori_loop(0, PEER_TOPK, pick, (s0, jnp.full((N_KEYS, tt), float(PEER_TOPK), F32)))
            rank_scr[hp] = rank
            return 0

        lax.fori_loop(0, 2 * PEER_HEADS, half_body, 0)

        n_cand = PEER_TOPK * PEER_TOPK
        cand_rows = lax.broadcasted_iota(jnp.int32, (n_cand, tt), 0).astype(F32)

        def head_body(h, _):
            t1 = top_scr[2 * h]
            t2 = top_scr[2 * h + 1]
            e1t = jnp.exp(t1 - t1[0:1])
            e2t = jnp.exp(t2 - t2[0:1])
            for a in range(PEER_TOPK):
                cand_scr[a * PEER_TOPK:(a + 1) * PEER_TOPK, :] = t1[a:a + 1] + t2
                w_scr[a * PEER_TOPK:(a + 1) * PEER_TOPK, :] = e1t[a:a + 1] * e2t

            def pick(r, carry):
                c, sel = carry
                c, hit = _take_top(c, cand_rows, n_cand, lambda m, hit: None)
                return c, jnp.where(hit, 1.0, sel)

            _, sel = lax.fori_loop(0, PEER_TOPK, pick, (cand_scr[...], jnp.zeros((n_cand, tt), F32)))
            z = jnp.sum(sel * w_scr[...], axis=0, keepdims=True)
            rank1 = rank_scr[2 * h]
            n_sel = jnp.zeros((N_KEYS, tt), F32)
            for a in range(PEER_TOPK):
                cnt = jnp.sum(sel[a * PEER_TOPK:(a + 1) * PEER_TOPK, :], axis=0, keepdims=True)
                n_sel = jnp.where(rank1 == float(a), cnt, n_sel)
            emit(h, rank_scr[2 * h + 1], n_sel, jnp.exp(sc_scr[2 * h] - t1[0:1]) / z,
                 jnp.exp(sc_scr[2 * h + 1] - t2[0:1]))
            return 0

        lax.fori_loop(0, PEER_HEADS, head_body, 0)


def _route(ya, yb, yc, x2d, lp, g, b):
    t = x2d.shape[0]
    tt = min(ROUTE_TILE, t)
    full = lambda a: pl.BlockSpec(a.shape, lambda i: (0,) * a.ndim)
    row = lambda w: pl.BlockSpec((tt, w), lambda i: (i, 0))
    route_spec = pl.BlockSpec((PEER_HEADS, N_KEYS, tt), lambda i: (0, 0, i))
    route_shape = lambda dt: jax.ShapeDtypeStruct((PEER_HEADS, N_KEYS, t), dt)
    n_cand = PEER_TOPK * PEER_TOPK
    return pl.pallas_call(
        _route_kernel,
        grid=(t // tt,),
        in_specs=[row(ya.shape[1]), row(yb.shape[1]), row(yc.shape[1]), row(D_MODEL), full(lp['w_out']),
                  full(g), full(b), full(lp['wq']), full(lp['keys'])],
        out_specs=[row(D_MODEL), row(D_MODEL), route_spec, route_spec, route_spec, route_spec],
        out_shape=[jax.ShapeDtypeStruct((t, D_MODEL), F32), jax.ShapeDtypeStruct((t, D_MODEL), BF16),
                   route_shape(BF16), route_shape(BF16), route_shape(F32), route_shape(F32)],
        scratch_shapes=[pltpu.VMEM((PEER_HEADS, tt, LANES), BF16),
                        pltpu.VMEM((2 * PEER_HEADS, N_KEYS, tt), F32),
                        pltpu.VMEM((2 * PEER_HEADS, N_KEYS, tt), F32),
                        pltpu.VMEM((2 * PEER_HEADS, PEER_TOPK, tt), F32),
                        pltpu.VMEM((n_cand, tt), F32),
                        pltpu.VMEM((n_cand, tt), F32)],
        compiler_params=_params("parallel"),
        name="route",
    )(ya, yb, yc, x2d, lp['w_out'], g, b, lp['wq'], lp['keys'])


def _peer_kernel(x1b_ref, u_ref, vt_ref, rb_ref, e2_ref, nb_ref, e1_ref, x1_ref, g_ref, b_ref,
                 o_ref, acc_ref, act_ref):
    ei = pl.program_id(1)
    tt = x1b_ref.shape[0]
    n_sub = u_ref.shape[0] // N_KEYS

    @pl.when(ei == 0)
    def _():
        acc_ref[...] = jnp.zeros_like(acc_ref)

    hid_t = lax.dot_general(u_ref[...], x1b_ref[...], NT_DIMS, preferred_element_type=F32)
    for ii in range(n_sub):
        i_row = ei * n_sub + ii
        gate = jnp.zeros((N_KEYS, tt), BF16)
        for h in range(PEER_HEADS):
            n_sel = jnp.broadcast_to(nb_ref[h, pl.ds(i_row, 1), :], (N_KEYS, tt)).astype(BF16)
            e1 = jnp.broadcast_to(e1_ref[h, pl.ds(i_row, 1), :], (N_KEYS, tt)).astype(BF16)
            gate = gate + jnp.where(rb_ref[h] < n_sel, e2_ref[h], jnp.zeros((), BF16)) * e1
        hid = hid_t[ii * N_KEYS:(ii + 1) * N_KEYS, :].astype(BF16)
        gelu = (hid * 0.5) * (1.0 + lax.erf(hid * RSQRT2))
        act_ref[ii * N_KEYS:(ii + 1) * N_KEYS, :] = gelu * gate
    acc_ref[...] += jnp.dot(vt_ref[...], act_ref[...], preferred_element_type=F32)

    @pl.when(ei == pl.num_programs(1) - 1)
    def _():
        ffn = acc_ref[...].T
        o_ref[...] = _layer_norm(DEEPNORM_ALPHA * x1_ref[...] + ffn, g_ref[...], b_ref[...])


def _peer(x1, x1b, rb, e2, nb, e1, lp, g, b):
    t = x1.shape[0]
    tt = min(PEER_TOKEN_TILE, t)
    eb = PEER_EXPERT_BLOCK
    full = lambda a: pl.BlockSpec(a.shape, lambda ti, ei: (0,) * a.ndim)
    route_spec = pl.BlockSpec((PEER_HEADS, N_KEYS, tt), lambda ti, ei: (0, 0, ti))
    return pl.pallas_call(
        _peer_kernel,
        grid=(t // tt, N_EXPERTS // eb),
        in_specs=[pl.BlockSpec((tt, D_MODEL), lambda ti, ei: (ti, 0)),
                  pl.BlockSpec((eb, D_MODEL), lambda ti, ei: (ei, 0)),
                  pl.BlockSpec((D_MODEL, eb), lambda ti, ei: (0, ei)),
                  route_spec, route_spec, route_spec, route_spec,
                  pl.BlockSpec((tt, D_MODEL), lambda ti, ei: (ti, 0)), full(g), full(b)],
        out_specs=pl.BlockSpec((tt, D_MODEL), lambda ti, ei: (ti, 0)),
        out_shape=jax.ShapeDtypeStruct((t, D_MODEL), F32),
        scratch_shapes=[pltpu.VMEM((D_MODEL, tt), F32), pltpu.VMEM((eb, tt), BF16)],
        compiler_params=_params("parallel", "arbitrary"),
        name="peer_experts",
    )(x1b, lp['u'], lp['v_t'], rb, e2, nb, e1, x1, g, b)


def kernel(x, w_in, mla_q_norm, mla_w_uq, mla_kv_norm, mla_w_ukv, rel_bias, swa_sinks, w_out, ln1_g, ln1_b,
           peer_wq, peer_keys, peer_u, peer_v, ln2_g, ln2_b):
    batch, seq, d = x.shape
    assert d == D_MODEL and seq % PROJ_TILE == 0 and PROJ_TILE % MLA_BLOCK == 0 and MLA_BLOCK % ATT_BLOCK == 0
    tabs = _make_tables(seq)
    h = x.reshape(batch * seq, d)
    for l in range(w_in.shape[0]):
        lp = _prep_layer(w_in[l], mla_w_uq[l], mla_w_ukv[l], rel_bias[l], w_out[l], peer_wq[l], peer_keys[l],
                         peer_u[l], peer_v[l])
        row = lambda a: a[l].reshape(1, -1)
        qa, ka, vat, qb, kb, vb, qc, kc, vc = _projections(h, lp, row(mla_q_norm), row(mla_kv_norm), tabs, seq)
        ya = _mla_attention(qa, ka, vat, batch, seq)
        yb = _rel_attention(qb, kb, vb, _rel_bias_tile(lp['rel_ext']), batch, seq)
        yc = _swa_attention(qc, kc, vc, swa_sinks[l], batch, seq)
        x1, x1b, rb, e2, nb, e1 = _route(ya, yb, yc, h, lp, row(ln1_g), row(ln1_b))
        h = _peer(x1, x1b, rb, e2, nb, e1, lp, row(ln2_g), row(ln2_b))
    return h.reshape(batch, seq, d)
```

```python
import numpy as np
import jax
import jax.numpy as jnp
from jax import lax
from jax.experimental import pallas as pl
from jax.experimental.pallas import tpu as pltpu

F32 = jnp.float32
BF16 = jnp.bfloat16

D_MODEL = 1024
DEPTH = 2
CHUNK = 64
ROPE_THETA = 10000.0
LN_EPS = 1e-5
RMS_EPS = 1e-6
NEG_INF = -1e30
DEEPNORM_ALPHA = (2.0 * DEPTH) ** 0.25

MLA_HEADS = 8
MLA_Q_RANK = 384
MLA_KV_RANK = 256
MLA_NOPE = 64
MLA_ROPE = 32
MLA_V = 64
MLA_DK = MLA_NOPE + MLA_ROPE

REL_HEADS = 4
REL_DIM = 64
REL_BACK_CHUNKS = 8
MAX_REL_DIST = 256

SWA_Q_HEADS = 4
SWA_KV_HEADS = 2
SWA_DIM = 64
SWA_BACK_CHUNKS = 2

IN_SPLITS = [MLA_Q_RANK, MLA_KV_RANK, MLA_ROPE, 256, 256, 256, 256, 128, 128]
IN_SPLIT_POINTS = [int(c) for c in np.cumsum(IN_SPLITS)[:-1]]

PEER_HEADS = 8
N_KEYS = 128
N_EXPERTS = N_KEYS * N_KEYS
PEER_HALF = 64
PEER_TOPK = 16

LANES = 128
SUBLANES = 8
HEAD_PAD = 128
SWA_HEAD_ORDER = (0, 2, 1, 3)

C_AQ, C_AKV, C_KR, C_KRR = 0, 384, 640, 768
C_BQ, C_BK, C_BV = 896, 1152, 1408
C_CQ, C_CQR, C_CK, C_CKR, C_CV = 1664, 1920, 2176, 2304, 2432

PROJ_TILE = 512
ATT_BLOCK = 256
MLA_BLOCK = 512
MLA_Q_SPLIT = 1
MLA_SUM_ROWS = 16
ROUTE_TILE = 512
PEER_TOKEN_TILE = 512
PEER_EXPERT_BLOCK = 2048
PEER_DOT_CHUNKS = 8
REL_EXT = 1024
VMEM_LIMIT = 56 * 1024 * 1024

NT_DIMS = (((1,), (1,)), ((), ()))
RSQRT2 = float(np.sqrt(0.5))


def _rot_half_cols(w, head_dim):
    k, c = w.shape
    w3 = w.reshape(k, c // head_dim, head_dim)
    half = head_dim // 2
    return jnp.concatenate([-w3[..., half:], w3[..., :half]], axis=-1).reshape(k, c)


def _place(w, lo, width=LANES):
    return jnp.pad(w, ((0, 0), (lo, width - lo - w.shape[1])))


def _rope_tables(seq, dim):
    inv = ROPE_THETA ** (-jnp.arange(0, dim, 2, dtype=F32) / dim)
    ang = jnp.arange(seq, dtype=F32)[:, None] * inv[None, :]
    return jnp.cos(ang), jnp.sin(ang)


def _make_tables(seq):
    c16, s16 = _rope_tables(seq, MLA_ROPE)
    c32, s32 = _rope_tables(seq, SWA_DIM)
    one = jnp.ones((seq, MLA_NOPE), F32)
    z64 = jnp.zeros((seq, MLA_NOPE), F32)
    z32 = jnp.zeros((seq, LANES - MLA_DK), F32)
    scale_a = MLA_DK ** -0.5
    cos_a = jnp.concatenate([one, c16, c16, z32], axis=1) * scale_a
    sin_a = jnp.concatenate([z64, s16, s16, z32], axis=1) * scale_a
    cos_k = jnp.concatenate([z64, c16, c16, z32], axis=1)
    sin_k = jnp.concatenate([z64, s16, s16, z32], axis=1)
    cos_c = jnp.concatenate([c32, c32, c32, c32], axis=1)
    sin_c = jnp.concatenate([s32, s32, s32, s32], axis=1)
    return jnp.stack([cos_a, sin_a, cos_k, sin_k, cos_c, sin_c])


def _prep_layer(w_in, w_uq, w_ukv, rel_bias, w_out, peer_wq, peer_keys, peer_u, peer_v):
    a_q, a_kv, a_kr, b_q, b_k, b_v, c_q, c_k, c_v = jnp.split(w_in, IN_SPLIT_POINTS, axis=1)
    c_q = c_q.reshape(D_MODEL, SWA_Q_HEADS, SWA_DIM)[:, SWA_HEAD_ORDER, :].reshape(D_MODEL, -1)
    att_scale = REL_DIM ** -0.5
    w_in_ext = jnp.concatenate([
        a_q, a_kv, _place(a_kr, MLA_NOPE), _place(_rot_half_cols(a_kr, MLA_ROPE), MLA_NOPE),
        b_q * att_scale, b_k, b_v,
        c_q * att_scale, _rot_half_cols(c_q, SWA_DIM) * att_scale, c_k, _rot_half_cols(c_k, SWA_DIM), c_v,
    ], axis=1).astype(BF16)

    uq = w_uq.reshape(MLA_Q_RANK, MLA_HEADS, MLA_DK)
    uq_pad = jnp.pad(uq, ((0, 0), (0, 0), (0, HEAD_PAD - MLA_DK))).reshape(MLA_Q_RANK, -1)
    uq_rope = uq[..., MLA_NOPE:]
    half = MLA_ROPE // 2
    uq_rot = jnp.concatenate([-uq_rope[..., half:], uq_rope[..., :half]], axis=-1)
    uq_rot = jnp.pad(uq_rot, ((0, 0), (0, 0), (MLA_NOPE, HEAD_PAD - MLA_DK))).reshape(MLA_Q_RANK, -1)
    w_uq_ext = jnp.concatenate([uq_pad, uq_rot], axis=1).astype(BF16)

    ukv = w_ukv.reshape(MLA_KV_RANK, MLA_HEADS, MLA_NOPE + MLA_V)
    w_uk = jnp.pad(ukv[..., :MLA_NOPE], ((0, 0), (0, 0), (0, HEAD_PAD - MLA_NOPE))).reshape(MLA_KV_RANK, -1).astype(BF16)
    w_uv_t = ukv[..., MLA_NOPE:].reshape(MLA_KV_RANK, -1).T.astype(BF16)

    m = jnp.arange(REL_EXT)
    ext_idx = jnp.clip(3 * ATT_BLOCK - m, -(CHUNK - 1), MAX_REL_DIST) + (CHUNK - 1)
    rel_ext = rel_bias[:, ext_idx].astype(F32).reshape(REL_HEADS, 1, REL_EXT)

    n_a = MLA_HEADS * MLA_V
    n_b = REL_HEADS * REL_DIM
    w_out_c = w_out[n_a + n_b:].reshape(SWA_Q_HEADS, SWA_DIM, D_MODEL)[SWA_HEAD_ORDER, :, :].reshape(-1, D_MODEL)
    w_out_ext = jnp.concatenate([w_out[:n_a + n_b], w_out_c], axis=0).astype(BF16)

    keys = peer_keys.reshape(PEER_HEADS * 2, N_KEYS, PEER_HALF)
    keys_lo = jnp.pad(keys, ((0, 0), (0, 0), (0, PEER_HALF)))
    keys_hi = jnp.pad(keys, ((0, 0), (0, 0), (PEER_HALF, 0)))
    is_hi = (jnp.arange(PEER_HEADS * 2) % 2 == 1)[:, None, None]
    keys_ext = jnp.where(is_hi, keys_hi, keys_lo).astype(BF16)

    return dict(w_in=w_in_ext, w_uq=w_uq_ext, w_uk=w_uk, w_uv_t=w_uv_t, rel_ext=rel_ext, w_out=w_out_ext,
                wq=peer_wq.astype(BF16), keys=keys_ext, u=peer_u.astype(BF16), v_t=peer_v.T.astype(BF16))


def _params(*sem):
    return pltpu.CompilerParams(dimension_semantics=sem, vmem_limit_bytes=VMEM_LIMIT)


def _layer_norm(z, g, b):
    mu = jnp.mean(z, axis=-1, keepdims=True)
    zc = z - mu
    var = jnp.mean(zc * zc, axis=-1, keepdims=True)
    return zc * lax.rsqrt(var + LN_EPS) * g + b


def _rms_norm(x, g):
    return x * lax.rsqrt(jnp.mean(x * x, axis=-1, keepdims=True) + RMS_EPS) * g


def _proj_kernel(x_ref, win_ref, qn_ref, wuq_ref, kvn_ref, wuk_ref, wuvt_ref, tab_ref,
                 qa_ref, ka_ref, vat_ref, qb_ref, kb_ref, vb_ref, qc_ref, kc_ref, vc_ref):
    xb = x_ref[...].astype(BF16)
    h_all = jnp.dot(xb, win_ref[...], preferred_element_type=F32)

    def proj(lo, width):
        return h_all[:, lo:lo + width]

    cos_a, sin_a, cos_k, sin_k, cos_c, sin_c = (tab_ref[i] for i in range(6))

    cqn = _rms_norm(proj(C_AQ, MLA_Q_RANK), qn_ref[...]).astype(BF16)
    rot0 = MLA_HEADS * HEAD_PAD
    q_all = jnp.dot(cqn, wuq_ref[...], preferred_element_type=F32)
    for h in range(MLA_HEADS):
        lo = h * HEAD_PAD
        qh = q_all[:, lo:lo + HEAD_PAD]
        qr = q_all[:, rot0 + lo:rot0 + lo + HEAD_PAD]
        qa_ref[:, lo:lo + HEAD_PAD] = (qh * cos_a + qr * sin_a).astype(BF16)

    ckvn = _rms_norm(proj(C_AKV, MLA_KV_RANK), kvn_ref[...]).astype(BF16)
    k_pe = proj(C_KR, LANES) * cos_k + proj(C_KRR, LANES) * sin_k
    k_all = jnp.dot(ckvn, wuk_ref[...], preferred_element_type=F32)
    for h in range(MLA_HEADS):
        lo = h * HEAD_PAD
        ka_ref[:, lo:lo + HEAD_PAD] = (k_all[:, lo:lo + HEAD_PAD] + k_pe).astype(BF16)
    for c in range(vat_ref.shape[0]):
        rows = ckvn[c * MLA_BLOCK:(c + 1) * MLA_BLOCK, :]
        vat_ref[c] = lax.dot_general(wuvt_ref[...], rows, NT_DIMS, preferred_element_type=F32).astype(BF16)

    qb_ref[...] = proj(C_BQ, 256).astype(BF16)
    kb_ref[...] = proj(C_BK, 256).astype(BF16)
    vb_ref[...] = proj(C_BV, 256).astype(BF16)

    for t in range(2):
        lo = t * LANES
        qc_ref[:, lo:lo + LANES] = (proj(C_CQ + lo, LANES) * cos_c + proj(C_CQR + lo, LANES) * sin_c).astype(BF16)
    kc_ref[...] = (proj(C_CK, LANES) * cos_c + proj(C_CKR, LANES) * sin_c).astype(BF16)
    vc_ref[...] = proj(C_CV, LANES).astype(BF16)


def _projections(x2d, lp, qn, kvn, tabs, seq):
    t = x2d.shape[0]
    tile = min(PROJ_TILE, seq)
    n_seq = seq // tile
    sub = tile // MLA_BLOCK
    full = lambda a: pl.BlockSpec(a.shape, lambda i: (0,) * a.ndim)
    row = lambda w: pl.BlockSpec((tile, w), lambda i: (i, 0))
    sds = lambda w: jax.ShapeDtypeStruct((t, w), BF16)
    n_v = MLA_HEADS * MLA_V
    vat_spec = pl.BlockSpec((sub, n_v, MLA_BLOCK), lambda i: (i, 0, 0))
    vat_sds = jax.ShapeDtypeStruct((t // MLA_BLOCK, n_v, MLA_BLOCK), BF16)
    widths = (256, 256, 256, 256, 128, 128)
    return pl.pallas_call(
        _proj_kernel,
        grid=(t // tile,),
        in_specs=[row(D_MODEL), full(lp['w_in']), full(qn), full(lp['w_uq']), full(kvn), full(lp['w_uk']),
                  full(lp['w_uv_t']), pl.BlockSpec((6, tile, LANES), lambda i: (0, i % n_seq, 0))],
        out_specs=[row(1024), row(1024), vat_spec] + [row(w) for w in widths],
        out_shape=[sds(1024), sds(1024), vat_sds] + [sds(w) for w in widths],
        compiler_params=_params("parallel"),
        name="projections",
    )(x2d, lp['w_in'], qn, lp['w_uq'], kvn, lp['w_uk'], lp['w_uv_t'], tabs)


def _mla_kernel(q_ref, k_ref, vt_ref, o_ref):
    i = pl.program_id(2)
    blk = q_ref.shape[0]
    qw = blk // MLA_Q_SPLIT
    chains = [(r, c) for r in range(2) for c in range(MLA_Q_SPLIT)]
    kchunk = lax.broadcasted_iota(jnp.int32, (blk, qw), 0) // CHUNK
    qchunk = lax.broadcasted_iota(jnp.int32, (blk, qw), 1) // CHUNK
    diag_visible = [kchunk <= qchunk + c * (qw // CHUNK) for c in range(MLA_Q_SPLIT)]
    qs = {(r, c): q_ref[c * qw:(c + 1) * qw, r * HEAD_PAD:(r + 1) * HEAD_PAD] for r, c in chains}
    ones_rows = (lax.broadcasted_iota(jnp.int32, (MLA_SUM_ROWS, blk), 0) == 0).astype(BF16)

    def step(j, carry, masked):
        start = pl.multiple_of(j * blk, blk)
        out = []
        scores = [lax.dot_general(k_ref[pl.ds(start, blk), r * HEAD_PAD:(r + 1) * HEAD_PAD], qs[r, c], NT_DIMS,
                                  preferred_element_type=F32) for r, c in chains]
        for n, (r, c) in enumerate(chains):
            m, acc = carry[n]
            s = scores[n]
            if masked:
                s = jnp.where(diag_visible[c], s, NEG_INF)
            m_new = jnp.maximum(m, jnp.max(s, axis=0, keepdims=True))
            p = jnp.exp((s - m_new).astype(BF16))
            a = jnp.exp(m - m_new)
            v_aug = jnp.concatenate([vt_ref[j, r * MLA_V:(r + 1) * MLA_V, :], ones_rows], axis=0)
            out.append((m_new, a * acc + jnp.dot(v_aug, p, preferred_element_type=F32)))
        return tuple(out)

    one = (jnp.full((1, qw), NEG_INF, F32), jnp.zeros((MLA_V + MLA_SUM_ROWS, qw), F32))
    carry = lax.fori_loop(0, i, lambda j, c: step(j, c, False), (one,) * len(chains))
    accs = [acc for _, acc in step(i, carry, True)]
    outs = [acc[:MLA_V] / acc[MLA_V:MLA_V + 1] for acc in accs]
    o_t = jnp.concatenate([jnp.concatenate(outs[r * MLA_Q_SPLIT:(r + 1) * MLA_Q_SPLIT], axis=1) for r in range(2)], axis=0)
    o_ref[...] = o_t.T.astype(BF16)


def _mla_attention(qa, ka, vat, batch, seq):
    blk = MLA_BLOCK
    nq = seq // blk
    pairs = MLA_HEADS // 2
    return pl.pallas_call(
        _mla_kernel,
        grid=(batch, pairs, nq),
        in_specs=[pl.BlockSpec((blk, 2 * HEAD_PAD), lambda b, p, i: (b * nq + i, p)),
                  pl.BlockSpec((seq, 2 * HEAD_PAD), lambda b, p, i: (b, p)),
                  pl.BlockSpec((nq, 2 * MLA_V, blk), lambda b, p, i: (b, p, 0))],
        out_specs=pl.BlockSpec((blk, 2 * MLA_V), lambda b, p, i: (b * nq + i, p)),
        out_shape=jax.ShapeDtypeStruct((batch * seq, MLA_HEADS * MLA_V), BF16),
        compiler_params=_params("parallel", "parallel", "arbitrary"),
        name="mla_attention",
    )(qa, ka, vat)


def _rel_bias_kernel(ext_ref, o_ref):
    blk = o_ref.shape[1]
    base = jnp.broadcast_to(ext_ref[0], (blk, REL_EXT))
    tile = pltpu.roll(base, REL_EXT - blk, 1, stride=1, stride_axis=0)[:, :3 * blk]
    qchunk = lax.broadcasted_iota(jnp.int32, (blk, 3 * blk), 0) // CHUNK + REL_BACK_CHUNKS
    kchunk = lax.broadcasted_iota(jnp.int32, (blk, 3 * blk), 1) // CHUNK
    visible = (kchunk >= qchunk - REL_BACK_CHUNKS) & (kchunk <= qchunk)
    o_ref[0] = jnp.where(visible, tile, NEG_INF)


def _rel_bias_tile(rel_ext):
    blk = ATT_BLOCK
    return pl.pallas_call(
        _rel_bias_kernel,
        grid=(REL_HEADS,),
        in_specs=[pl.BlockSpec((1, 1, REL_EXT), lambda h: (h, 0, 0))],
        out_specs=pl.BlockSpec((1, blk, 3 * blk), lambda h: (h, 0, 0)),
        out_shape=jax.ShapeDtypeStruct((REL_HEADS, blk, 3 * blk), F32),
        compiler_params=_params("parallel"),
        name="rel_bias_tile",
    )(rel_ext)


def _rel_kernel(q_ref, k0_ref, k1_ref, k2_ref, v0_ref, v1_ref, v2_ref, bias_ref, o_ref):
    i = pl.program_id(2)
    blk = q_ref.shape[0]
    kwin = jnp.concatenate([k0_ref[...], k1_ref[...], k2_ref[...]], axis=0)
    vwin = jnp.concatenate([v0_ref[...], v1_ref[...], v2_ref[...]], axis=0)
    col = lax.broadcasted_iota(jnp.int32, (1, 3 * blk), 1)
    in_seq = col >= (2 - i) * blk
    left = lax.broadcasted_iota(jnp.int32, (1, LANES), 1) < REL_DIM
    q = q_ref[...]
    res = None
    head_q = [jnp.where(left if r == 0 else jnp.logical_not(left), q, jnp.zeros_like(q)) for r in range(2)]
    scores = [lax.dot_general(qm, kwin, NT_DIMS, preferred_element_type=F32) for qm in head_q]
    for r in range(2):
        s = jnp.where(in_seq, scores[r] + bias_ref[r], NEG_INF)
        m = jnp.max(s, axis=-1, keepdims=True)
        p = jnp.exp(s - m)
        l = jnp.sum(p, axis=-1, keepdims=True)
        o = jnp.dot(p.astype(BF16), vwin, preferred_element_type=F32) / l
        res = o if r == 0 else jnp.where(left, res, o)
    o_ref[...] = res.astype(BF16)


def _rel_attention(qb, kb, vb, bias, batch, seq):
    blk = ATT_BLOCK
    nq = seq // blk
    pairs = REL_HEADS // 2
    back = lambda d: (lambda b, p, i: (b * nq + jnp.maximum(i - d, 0), p))
    tile = lambda d: pl.BlockSpec((blk, LANES), back(d))
    return pl.pallas_call(
        _rel_kernel,
        grid=(batch, pairs, nq),
        in_specs=[tile(0), tile(2), tile(1), tile(0), tile(2), tile(1), tile(0),
                  pl.BlockSpec((2, blk, 3 * blk), lambda b, p, i: (p, 0, 0))],
        out_specs=tile(0),
        out_shape=jax.ShapeDtypeStruct((batch * seq, REL_HEADS * REL_DIM), BF16),
        compiler_params=_params("parallel", "parallel", "arbitrary"),
        name="rel_attention",
    )(qb, kb, kb, kb, vb, vb, vb, bias)


def _swa_kernel(sink_ref, q_ref, kp_ref, kc_ref, vp_ref, vc_ref, o_ref):
    tile = pl.program_id(1)
    i = pl.program_id(2)
    blk = q_ref.shape[0]
    back = SWA_BACK_CHUNKS * CHUNK
    kwin = jnp.concatenate([kp_ref[blk - back:, :], kc_ref[...]], axis=0)
    vwin = jnp.concatenate([vp_ref[blk - back:, :], vc_ref[...]], axis=0)
    qchunk = lax.broadcasted_iota(jnp.int32, (blk, blk + back), 0) // CHUNK
    col = lax.broadcasted_iota(jnp.int32, (blk, blk + back), 1)
    kchunk = col // CHUNK
    visible = (kchunk >= qchunk) & (kchunk <= qchunk + SWA_BACK_CHUNKS) & ((col >= back) | (i > 0))
    left = lax.broadcasted_iota(jnp.int32, (1, LANES), 1) < SWA_DIM
    q = q_ref[...]
    res = None
    head_q = [jnp.where(left if r == 0 else jnp.logical_not(left), q, jnp.zeros_like(q)) for r in range(2)]
    scores = [lax.dot_general(qm, kwin, NT_DIMS, preferred_element_type=F32) for qm in head_q]
    for r in range(2):
        s = jnp.where(visible, scores[r], NEG_INF)
        sink = sink_ref[2 * r + tile]
        m = jnp.maximum(jnp.max(s, axis=-1, keepdims=True), sink)
        p = jnp.exp(s - m)
        l = jnp.sum(p, axis=-1, keepdims=True) + jnp.exp(sink - m)
        o = jnp.dot(p.astype(BF16), vwin, preferred_element_type=F32) / l
        res = o if r == 0 else jnp.where(left, res, o)
    o_ref[...] = res.astype(BF16)


def _swa_attention(qc, kc, vc, sinks, batch, seq):
    blk = ATT_BLOCK
    nq = seq // blk
    cur = lambda b, t, i: (b * nq + i, 0)
    prev = lambda b, t, i: (b * nq + jnp.maximum(i - 1, 0), 0)
    return pl.pallas_call(
        _swa_kernel,
        grid=(batch, 2, nq),
        in_specs=[pl.BlockSpec(memory_space=pltpu.SMEM),
                  pl.BlockSpec((blk, LANES), lambda b, t, i: (b * nq + i, t)),
                  pl.BlockSpec((blk, LANES), prev), pl.BlockSpec((blk, LANES), cur),
                  pl.BlockSpec((blk, LANES), prev), pl.BlockSpec((blk, LANES), cur)],
        out_specs=pl.BlockSpec((blk, LANES), lambda b, t, i: (b * nq + i, t)),
        out_shape=jax.ShapeDtypeStruct((batch * seq, SWA_Q_HEADS * SWA_DIM), BF16),
        compiler_params=_params("parallel", "parallel", "arbitrary"),
        name="swa_attention",
    )(sinks, qc, kc, kc, vc, vc)


def _oddeven_merge_sort_pairs(n):
    pairs = []

    def merge(lo, hi, r):
        step = r * 2
        if step < hi - lo:
            merge(lo, hi, step)
            merge(lo + r, hi, step)
            pairs.extend((i, i + r) for i in range(lo + r, hi - r, step))
        else:
            pairs.append((lo, lo + r))

    def sort(lo, hi):
        if hi - lo >= 1:
            mid = lo + (hi - lo) // 2
            sort(lo, mid)
            sort(mid + 1, hi)
            merge(lo, hi, 1)

    sort(0, n - 1)
    return pairs


def _bitonic_merge_pairs(n):
    pairs = []
    s = n // 2
    while s >= 1:
        pairs.extend((i, i + s) for i in range(n) if (i // s) % 2 == 0)
        s //= 2
    return pairs


SORT16 = _oddeven_merge_sort_pairs(PEER_TOPK)
BITONIC16 = _bitonic_merge_pairs(PEER_TOPK)
STAIRCASE = [(a, b) for a in range(1, PEER_TOPK) for b in range(PEER_TOPK) if (a + 1) * (b + 1) <= PEER_TOPK]


def _compare_exchange(vals, pairs):
    vals = list(vals)
    for a, b in pairs:
        hi = jnp.maximum(vals[a], vals[b])
        lo = jnp.minimum(vals[a], vals[b])
        vals[a], vals[b] = hi, lo
    return vals


def _merge_top16(xs, ys):
    return _compare_exchange([jnp.maximum(xs[i], ys[PEER_TOPK - 1 - i]) for i in range(PEER_TOPK)], BITONIC16)


def _sorted_top16(s):
    slabs = _compare_exchange([s[SUBLANES * v:SUBLANES * (v + 1), :] for v in range(N_KEYS // SUBLANES)], SORT16)
    for shift in (4, 2, 1):
        slabs = _merge_top16(slabs, [pltpu.roll(x, shift, 0) for x in slabs])
    return slabs


def _count_prefix(test, vals):
    cands = list(vals[:PEER_TOPK - 1])
    count = None
    step = PEER_TOPK // 2
    while step >= 1:
        c = test(cands[step - 1])
        inc = jnp.where(c, float(step), 0.0)
        count = inc if count is None else count + inc
        cands = [jnp.where(c, cands[k + step], cands[k]) for k in range(step - 1)]
        step //= 2
    return count + jnp.where(test(vals[PEER_TOPK - 1]), 1.0, 0.0)


def _take_top(s, rows, n_rows, on_pick):
    m = jnp.max(s, axis=0, keepdims=True)
    idx = jnp.min(jnp.where(s == m, rows, float(n_rows)), axis=0, keepdims=True)
    hit = rows == idx
    on_pick(m, hit)
    return jnp.where(hit, -jnp.inf, s), hit


def _route_kernel(ya_ref, yb_ref, yc_ref, x_ref, wout_ref, g_ref, b_ref, wq_ref, keys_ref,
                  x1_ref, x1b_ref, rb_ref, e2_ref, nb_ref, e1_ref,
                  q_scr, sc_scr, rank_scr, top_scr, cand_scr, w_scr):
    tt = x_ref.shape[0]
    n_a, n_b = ya_ref.shape[1], yb_ref.shape[1]
    mix = (jnp.dot(ya_ref[...], wout_ref[0:n_a, :], preferred_element_type=F32)
           + jnp.dot(yb_ref[...], wout_ref[n_a:n_a + n_b, :], preferred_element_type=F32)
           + jnp.dot(yc_ref[...], wout_ref[n_a + n_b:, :], preferred_element_type=F32))
    x1 = _layer_norm(DEEPNORM_ALPHA * x_ref[...] + mix, g_ref[...], b_ref[...])
    x1_ref[...] = x1
    x1b = x1.astype(BF16)
    x1b_ref[...] = x1b
    q = jnp.dot(x1b, wq_ref[...], preferred_element_type=F32).astype(BF16)
    for h in range(PEER_HEADS):
        q_scr[h] = q[:, h * LANES:(h + 1) * LANES]

    def emit(h, rank2, n_sel, e1, e2):
        rb_ref[h] = rank2.astype(BF16)
        e2_ref[h] = e2.astype(BF16)
        nb_ref[h] = n_sel
        e1_ref[h] = e1

    n_slab = N_KEYS // SUBLANES

    def fast_head(h, bad):
        s1 = lax.dot_general(keys_ref[2 * h], q_scr[h], NT_DIMS, preferred_element_type=F32)
        s2 = lax.dot_general(keys_ref[2 * h + 1], q_scr[h], NT_DIMS, preferred_element_type=F32)
        sc_scr[2 * h] = s1
        sc_scr[2 * h + 1] = s2
        t1 = _sorted_top16(s1)
        t2 = _sorted_top16(s2)
        neg = jnp.full_like(t1[0], -jnp.inf)
        rest = [t1[a] + t2[b] for a, b in STAIRCASE]
        rest += [neg] * (3 * PEER_TOPK - len(rest))
        groups = [_compare_exchange(rest[g * PEER_TOPK:(g + 1) * PEER_TOPK], SORT16) for g in range(3)]
        row0 = [t1[0] + t2[b] for b in range(PEER_TOPK)]
        ctop = _merge_top16(_merge_top16(row0, groups[0]), _merge_top16(groups[1], groups[2]))
        tau = ctop[PEER_TOPK - 1]
        z = jnp.exp(ctop[0] - ctop[0])
        for r in range(1, PEER_TOPK):
            z = z + jnp.exp(ctop[r] - ctop[0])
        inv_z = 1.0 / z
        in_top = jnp.zeros_like(tau)
        sel_mass = jnp.zeros_like(tau)
        ranks, counts, e1s, e2s = [], [], [], []
        for v in range(n_slab):
            a1 = s1[SUBLANES * v:SUBLANES * (v + 1), :]
            a2 = s2[SUBLANES * v:SUBLANES * (v + 1), :]
            rank2 = _count_prefix(lambda top: top > a2, t2)
            n_sel = _count_prefix(lambda top: a1 + top >= tau, t2)
            in_top = in_top + jnp.minimum(float(PEER_TOPK) - rank2, 1.0)
            sel_mass = sel_mass + n_sel
            ranks.append(rank2)
            counts.append(n_sel)
            e1s.append(jnp.exp(a1 - t1[0]) * inv_z)
            e2s.append(jnp.exp(a2 - t2[0]))
        emit(h, jnp.concatenate(ranks, axis=0), jnp.concatenate(counts, axis=0),
             jnp.concatenate(e1s, axis=0), jnp.concatenate(e2s, axis=0))
        in_top = jnp.sum(in_top, axis=0, keepdims=True)
        sel_mass = jnp.sum(sel_mass, axis=0, keepdims=True)
        tied = (in_top != float(PEER_TOPK)) | (sel_mass != float(PEER_TOPK))
        return jnp.maximum(bad, jnp.where(tied, 1.0, 0.0))

    bad = lax.fori_loop(0, PEER_HEADS, fast_head, jnp.zeros((1, tt), F32))

    @pl.when(jnp.max(bad) > 0.0)
    def _exact_with_ties():
        key_rows = lax.broadcasted_iota(jnp.int32, (N_KEYS, tt), 0).astype(F32)

        def half_body(hp, _):
            s0 = sc_scr[hp]

            def pick(r, carry):
                s, rank = carry

                def on_pick(m, hit):
                    top_scr[hp, pl.ds(r, 1), :] = m

                s, hit = _take_top(s, key_rows, N_KEYS, on_pick)
                return s, jnp.where(hit, r.astype(F32), rank)

            _, rank = lax.fori_loop(0, PEER_TOPK, pick, (s0, jnp.full((N_KEYS, tt), float(PEER_TOPK), F32)))
            rank_scr[hp] = rank
            return 0

        lax.fori_loop(0, 2 * PEER_HEADS, half_body, 0)

        n_cand = PEER_TOPK * PEER_TOPK
        cand_rows = lax.broadcasted_iota(jnp.int32, (n_cand, tt), 0).astype(F32)

        def head_body(h, _):
            t1 = top_scr[2 * h]
            t2 = top_scr[2 * h + 1]
            e1t = jnp.exp(t1 - t1[0:1])
            e2t = jnp.exp(t2 - t2[0:1])
            for a in range(PEER_TOPK):
                cand_scr[a * PEER_TOPK:(a + 1) * PEER_TOPK, :] = t1[a:a + 1] + t2
                w_scr[a * PEER_TOPK:(a + 1) * PEER_TOPK, :] = e1t[a:a + 1] * e2t

            def pick(r, carry):
                c, sel = carry
                c, hit = _take_top(c, cand_rows, n_cand, lambda m, hit: None)
                return c, jnp.where(hit, 1.0, sel)

            _, sel = lax.fori_loop(0, PEER_TOPK, pick, (cand_scr[...], jnp.zeros((n_cand, tt), F32)))
            z = jnp.sum(sel * w_scr[...], axis=0, keepdims=True)
            rank1 = rank_scr[2 * h]
            n_sel = jnp.zeros((N_KEYS, tt), F32)
            for a in range(PEER_TOPK):
                cnt = jnp.sum(sel[a * PEER_TOPK:(a + 1) * PEER_TOPK, :], axis=0, keepdims=True)
                n_sel = jnp.where(rank1 == float(a), cnt, n_sel)
            emit(h, rank_scr[2 * h + 1], n_sel, jnp.exp(sc_scr[2 * h] - t1[0:1]) / z,
                 jnp.exp(sc_scr[2 * h + 1] - t2[0:1]))
            return 0

        lax.fori_loop(0, PEER_HEADS, head_body, 0)


def _route(ya, yb, yc, x2d, lp, g, b):
    t = x2d.shape[0]
    tt = min(ROUTE_TILE, t)
    full = lambda a: pl.BlockSpec(a.shape, lambda i: (0,) * a.ndim)
    row = lambda w: pl.BlockSpec((tt, w), lambda i: (i, 0))
    route_spec = pl.BlockSpec((PEER_HEADS, N_KEYS, tt), lambda i: (0, 0, i))
    route_shape = lambda dt: jax.ShapeDtypeStruct((PEER_HEADS, N_KEYS, t), dt)
    n_cand = PEER_TOPK * PEER_TOPK
    return pl.pallas_call(
        _route_kernel,
        grid=(t // tt,),
        in_specs=[row(ya.shape[1]), row(yb.shape[1]), row(yc.shape[1]), row(D_MODEL), full(lp['w_out']),
                  full(g), full(b), full(lp['wq']), full(lp['keys'])],
        out_specs=[row(D_MODEL), row(D_MODEL), route_spec, route_spec, route_spec, route_spec],
        out_shape=[jax.ShapeDtypeStruct((t, D_MODEL), F32), jax.ShapeDtypeStruct((t, D_MODEL), BF16),
                   route_shape(BF16), route_shape(BF16), route_shape(F32), route_shape(F32)],
        scratch_shapes=[pltpu.VMEM((PEER_HEADS, tt, LANES), BF16),
                        pltpu.VMEM((2 * PEER_HEADS, N_KEYS, tt), F32),
                        pltpu.VMEM((2 * PEER_HEADS, N_KEYS, tt), F32),
                        pltpu.VMEM((2 * PEER_HEADS, PEER_TOPK, tt), F32),
                        pltpu.VMEM((n_cand, tt), F32),
                        pltpu.VMEM((n_cand, tt), F32)],
        compiler_params=_params("parallel"),
        name="route",
    )(ya, yb, yc, x2d, lp['w_out'], g, b, lp['wq'], lp['keys'])


def _peer_kernel(x1b_ref, u_ref, vt_ref, rb_ref, e2_ref, nb_ref, e1_ref, x1_ref, g_ref, b_ref,
                 o_ref, acc_ref, act_ref):
    ei = pl.program_id(1)
    tt = x1b_ref.shape[0]
    n_sub = u_ref.shape[0] // N_KEYS

    @pl.when(ei == 0)
    def _():
        acc_ref[...] = jnp.zeros_like(acc_ref)

    def gate_rows(ii):
        i_row = ei * n_sub + ii
        gate = jnp.zeros((N_KEYS, tt), BF16)
        for h in range(PEER_HEADS):
            n_sel = jnp.broadcast_to(nb_ref[h, pl.ds(i_row, 1), :], (N_KEYS, tt)).astype(BF16)
            e1 = jnp.broadcast_to(e1_ref[h, pl.ds(i_row, 1), :], (N_KEYS, tt)).astype(BF16)
            gate = gate + jnp.where(rb_ref[h] < n_sel, e2_ref[h], jnp.zeros((), BF16)) * e1
        return gate

    per = n_sub // PEER_DOT_CHUNKS
    chunk_rows = per * N_KEYS
    gates = {}
    hids = []
    for c in range(PEER_DOT_CHUNKS):
        lhs = u_ref[c * chunk_rows:(c + 1) * chunk_rows, :]
        if c > 0:
            for ii in range((c - 1) * per, c * per):
                gates[ii] = gate_rows(ii)
            tile = gates[c * per - 1][0:2 * SUBLANES, 0:LANES]
            zero = pltpu.bitcast((pltpu.bitcast(tile, jnp.uint32) >> 16) >> 16, tile.dtype)
            head = lhs[0:2 * SUBLANES, :] + jnp.concatenate([zero] * (D_MODEL // LANES), axis=1)
            lhs = jnp.concatenate([head, lhs[2 * SUBLANES:, :]], axis=0)
        hids.append(lax.dot_general(lhs, x1b_ref[...], NT_DIMS, preferred_element_type=F32))
    for ii in range((PEER_DOT_CHUNKS - 1) * per, n_sub):
        gates[ii] = gate_rows(ii)
    for ii in range(n_sub):
        hid = hids[ii // per][(ii % per) * N_KEYS:(ii % per + 1) * N_KEYS, :].astype(BF16)
        gelu = (hid * 0.5) * (1.0 + lax.erf(hid * RSQRT2))
        act_ref[ii * N_KEYS:(ii + 1) * N_KEYS, :] = gelu * gates[ii]
    acc_ref[...] += jnp.dot(vt_ref[...], act_ref[...], preferred_element_type=F32)

    @pl.when(ei == pl.num_programs(1) - 1)
    def _():
        ffn = acc_ref[...].T
        o_ref[...] = _layer_norm(DEEPNORM_ALPHA * x1_ref[...] + ffn, g_ref[...], b_ref[...])


def _peer(x1, x1b, rb, e2, nb, e1, lp, g, b):
    t = x1.shape[0]
    tt = min(PEER_TOKEN_TILE, t)
    eb = PEER_EXPERT_BLOCK
    full = lambda a: pl.BlockSpec(a.shape, lambda ti, ei: (0,) * a.ndim)
    route_spec = pl.BlockSpec((PEER_HEADS, N_KEYS, tt), lambda ti, ei: (0, 0, ti))
    return pl.pallas_call(
        _peer_kernel,
        grid=(t // tt, N_EXPERTS // eb),
        in_specs=[pl.BlockSpec((tt, D_MODEL), lambda ti, ei: (ti, 0)),
                  pl.BlockSpec((eb, D_MODEL), lambda ti, ei: (ei, 0)),
                  pl.BlockSpec((D_MODEL, eb), lambda ti, ei: (0, ei)),
                  route_spec, route_spec, route_spec, route_spec,
                  pl.BlockSpec((tt, D_MODEL), lambda ti, ei: (ti, 0)), full(g), full(b)],
        out_specs=pl.BlockSpec((tt, D_MODEL), lambda ti, ei: (ti, 0)),
        out_shape=jax.ShapeDtypeStruct((t, D_MODEL), F32),
        scratch_shapes=[pltpu.VMEM((D_MODEL, tt), F32), pltpu.VMEM((eb, tt), BF16)],
        compiler_params=_params("parallel", "arbitrary"),
        name="peer_experts",
    )(x1b, lp['u'], lp['v_t'], rb, e2, nb, e1, x1, g, b)


def kernel(x, w_in, mla_q_norm, mla_w_uq, mla_kv_norm, mla_w_ukv, rel_bias, swa_sinks, w_out, ln1_g, ln1_b,
           peer_wq, peer_keys, peer_u, peer_v, ln2_g, ln2_b):
    batch, seq, d = x.shape
    assert d == D_MODEL and seq % PROJ_TILE == 0 and PROJ_TILE % MLA_BLOCK == 0 and MLA_BLOCK % ATT_BLOCK == 0
    tabs = _make_tables(seq)
    h = x.reshape(batch * seq, d)
    for l in range(w_in.shape[0]):
        lp = _prep_layer(w_in[l], mla_w_uq[l], mla_w_ukv[l], rel_bias[l], w_out[l], peer_wq[l], peer_keys[l],
                         peer_u[l], peer_v[l])
        row = lambda a: a[l].reshape(1, -1)
        qa, ka, vat, qb, kb, vb, qc, kc, vc = _projections(h, lp, row(mla_q_norm), row(mla_kv_norm), tabs, seq)
        ya = _mla_attention(qa, ka, vat, batch, seq)
        yb = _rel_attention(qb, kb, vb, _rel_bias_tile(lp['rel_ext']), batch, seq)
        yc = _swa_attention(qc, kc, vc, swa_sinks[l], batch, seq)
        x1, x1b, rb, e2, nb, e1 = _route(ya, yb, yc, h, lp, row(ln1_g), row(ln1_b))
        h = _peer(x1, x1b, rb, e2, nb, e1, lp, row(ln2_g), row(ln2_b))
    return h.reshape(batch, seq, d)
```

```python
import numpy as np
import jax
import jax.numpy as jnp
from jax import lax
from jax.experimental import pallas as pl
from jax.experimental.pallas import tpu as pltpu

F32 = jnp.float32
BF16 = jnp.bfloat16

D_MODEL = 1024
DEPTH = 2
CHUNK = 64
ROPE_THETA = 10000.0
LN_EPS = 1e-5
RMS_EPS = 1e-6
NEG_INF = -1e30
DEEPNORM_ALPHA = (2.0 * DEPTH) ** 0.25

MLA_HEADS = 8
MLA_Q_RANK = 384
MLA_KV_RANK = 256
MLA_NOPE = 64
MLA_ROPE = 32
MLA_V = 64
MLA_DK = MLA_NOPE + MLA_ROPE

REL_HEADS = 4
REL_DIM = 64
REL_BACK_CHUNKS = 8
MAX_REL_DIST = 256

SWA_Q_HEADS = 4
SWA_KV_HEADS = 2
SWA_DIM = 64
SWA_BACK_CHUNKS = 2

IN_SPLITS = [MLA_Q_RANK, MLA_KV_RANK, MLA_ROPE, 256, 256, 256, 256, 128, 128]
IN_SPLIT_POINTS = [int(c) for c in np.cumsum(IN_SPLITS)[:-1]]

PEER_HEADS = 8
N_KEYS = 128
N_EXPERTS = N_KEYS * N_KEYS
PEER_HALF = 64
PEER_TOPK = 16

LANES = 128
SUBLANES = 8
HEAD_PAD = 128
SWA_HEAD_ORDER = (0, 2, 1, 3)

C_AQ, C_AKV, C_KR, C_KRR = 0, 384, 640, 768
C_BQ, C_BK, C_BV = 896, 1152, 1408
C_CQ, C_CQR, C_CK, C_CKR, C_CV = 1664, 1920, 2176, 2304, 2432

PROJ_TILE = 512
ATT_BLOCK = 256
MLA_BLOCK = 512
MLA_SUM_ROWS = 16
ROUTE_TILE = 512
PEER_TOKEN_TILE = 512
PEER_EXPERT_BLOCK = 2048
PEER_DOT_CHUNKS = 8
PEER_PACE_LAG = 1
REL_EXT = 1024
VMEM_LIMIT = 56 * 1024 * 1024

NT_DIMS = (((1,), (1,)), ((), ()))
RSQRT2 = float(np.sqrt(0.5))


def _rot_half_cols(w, head_dim):
    k, c = w.shape
    w3 = w.reshape(k, c // head_dim, head_dim)
    half = head_dim // 2
    return jnp.concatenate([-w3[..., half:], w3[..., :half]], axis=-1).reshape(k, c)


def _place(w, lo, width=LANES):
    return jnp.pad(w, ((0, 0), (lo, width - lo - w.shape[1])))


def _rope_tables(seq, dim):
    inv = ROPE_THETA ** (-jnp.arange(0, dim, 2, dtype=F32) / dim)
    ang = jnp.arange(seq, dtype=F32)[:, None] * inv[None, :]
    return jnp.cos(ang), jnp.sin(ang)


def _make_tables(seq):
    c16, s16 = _rope_tables(seq, MLA_ROPE)
    c32, s32 = _rope_tables(seq, SWA_DIM)
    one = jnp.ones((seq, MLA_NOPE), F32)
    z64 = jnp.zeros((seq, MLA_NOPE), F32)
    z32 = jnp.zeros((seq, LANES - MLA_DK), F32)
    scale_a = MLA_DK ** -0.5
    cos_a = jnp.concatenate([one, c16, c16, z32], axis=1) * scale_a
    sin_a = jnp.concatenate([z64, s16, s16, z32], axis=1) * scale_a
    cos_k = jnp.concatenate([z64, c16, c16, z32], axis=1)
    sin_k = jnp.concatenate([z64, s16, s16, z32], axis=1)
    cos_c = jnp.concatenate([c32, c32, c32, c32], axis=1)
    sin_c = jnp.concatenate([s32, s32, s32, s32], axis=1)
    return jnp.stack([cos_a, sin_a, cos_k, sin_k, cos_c, sin_c])


def _prep_layer(w_in, w_uq, w_ukv, rel_bias, w_out, peer_wq, peer_keys, peer_u, peer_v):
    a_q, a_kv, a_kr, b_q, b_k, b_v, c_q, c_k, c_v = jnp.split(w_in, IN_SPLIT_POINTS, axis=1)
    c_q = c_q.reshape(D_MODEL, SWA_Q_HEADS, SWA_DIM)[:, SWA_HEAD_ORDER, :].reshape(D_MODEL, -1)
    att_scale = REL_DIM ** -0.5
    w_in_ext = jnp.concatenate([
        a_q, a_kv, _place(a_kr, MLA_NOPE), _place(_rot_half_cols(a_kr, MLA_ROPE), MLA_NOPE),
        b_q * att_scale, b_k, b_v,
        c_q * att_scale, _rot_half_cols(c_q, SWA_DIM) * att_scale, c_k, _rot_half_cols(c_k, SWA_DIM), c_v,
    ], axis=1).astype(BF16)

    uq = w_uq.reshape(MLA_Q_RANK, MLA_HEADS, MLA_DK)
    uq_pad = jnp.pad(uq, ((0, 0), (0, 0), (0, HEAD_PAD - MLA_DK))).reshape(MLA_Q_RANK, -1)
    uq_rope = uq[..., MLA_NOPE:]
    half = MLA_ROPE // 2
    uq_rot = jnp.concatenate([-uq_rope[..., half:], uq_rope[..., :half]], axis=-1)
    uq_rot = jnp.pad(uq_rot, ((0, 0), (0, 0), (MLA_NOPE, HEAD_PAD - MLA_DK))).reshape(MLA_Q_RANK, -1)
    w_uq_ext = jnp.concatenate([uq_pad, uq_rot], axis=1).astype(BF16)

    ukv = w_ukv.reshape(MLA_KV_RANK, MLA_HEADS, MLA_NOPE + MLA_V)
    w_uk = jnp.pad(ukv[..., :MLA_NOPE], ((0, 0), (0, 0), (0, HEAD_PAD - MLA_NOPE))).reshape(MLA_KV_RANK, -1).astype(BF16)
    w_uv_t = ukv[..., MLA_NOPE:].reshape(MLA_KV_RANK, -1).T.astype(BF16)

    m = jnp.arange(REL_EXT)
    ext_idx = jnp.clip(3 * ATT_BLOCK - m, -(CHUNK - 1), MAX_REL_DIST) + (CHUNK - 1)
    rel_ext = rel_bias[:, ext_idx].astype(F32).reshape(REL_HEADS, 1, REL_EXT)

    n_a = MLA_HEADS * MLA_V
    n_b = REL_HEADS * REL_DIM
    w_out_c = w_out[n_a + n_b:].reshape(SWA_Q_HEADS, SWA_DIM, D_MODEL)[SWA_HEAD_ORDER, :, :].reshape(-1, D_MODEL)
    w_out_ext = jnp.concatenate([w_out[:n_a + n_b], w_out_c], axis=0).astype(BF16)

    keys = peer_keys.reshape(PEER_HEADS * 2, N_KEYS, PEER_HALF)
    keys_lo = jnp.pad(keys, ((0, 0), (0, 0), (0, PEER_HALF)))
    keys_hi = jnp.pad(keys, ((0, 0), (0, 0), (PEER_HALF, 0)))
    is_hi = (jnp.arange(PEER_HEADS * 2) % 2 == 1)[:, None, None]
    keys_ext = jnp.where(is_hi, keys_hi, keys_lo).astype(BF16)

    return dict(w_in=w_in_ext, w_uq=w_uq_ext, w_uk=w_uk, w_uv_t=w_uv_t, rel_ext=rel_ext, w_out=w_out_ext,
                wq=peer_wq.astype(BF16), keys=keys_ext, u=peer_u.astype(BF16), v_t=peer_v.T.astype(BF16))


def _params(*sem):
    return pltpu.CompilerParams(dimension_semantics=sem, vmem_limit_bytes=VMEM_LIMIT)


def _layer_norm(z, g, b):
    mu = jnp.mean(z, axis=-1, keepdims=True)
    zc = z - mu
    var = jnp.mean(zc * zc, axis=-1, keepdims=True)
    return zc * lax.rsqrt(var + LN_EPS) * g + b


def _rms_norm(x, g):
    return x * lax.rsqrt(jnp.mean(x * x, axis=-1, keepdims=True) + RMS_EPS) * g


def _proj_kernel(x_ref, win_ref, qn_ref, wuq_ref, kvn_ref, wuk_ref, wuvt_ref, tab_ref,
                 qa_ref, ka_ref, vat_ref, qb_ref, kb_ref, vb_ref, qc_ref, kc_ref, vc_ref):
    xb = x_ref[...].astype(BF16)
    h_all = jnp.dot(xb, win_ref[...], preferred_element_type=F32)

    def proj(lo, width):
        return h_all[:, lo:lo + width]

    cos_a, sin_a, cos_k, sin_k, cos_c, sin_c = (tab_ref[i] for i in range(6))

    cqn = _rms_norm(proj(C_AQ, MLA_Q_RANK), qn_ref[...]).astype(BF16)
    rot0 = MLA_HEADS * HEAD_PAD
    q_all = jnp.dot(cqn, wuq_ref[...], preferred_element_type=F32)
    for h in range(MLA_HEADS):
        lo = h * HEAD_PAD
        qh = q_all[:, lo:lo + HEAD_PAD]
        qr = q_all[:, rot0 + lo:rot0 + lo + HEAD_PAD]
        qa_ref[:, lo:lo + HEAD_PAD] = (qh * cos_a + qr * sin_a).astype(BF16)

    ckvn = _rms_norm(proj(C_AKV, MLA_KV_RANK), kvn_ref[...]).astype(BF16)
    k_pe = proj(C_KR, LANES) * cos_k + proj(C_KRR, LANES) * sin_k
    k_all = jnp.dot(ckvn, wuk_ref[...], preferred_element_type=F32)
    for h in range(MLA_HEADS):
        lo = h * HEAD_PAD
        ka_ref[:, lo:lo + HEAD_PAD] = (k_all[:, lo:lo + HEAD_PAD] + k_pe).astype(BF16)
    for c in range(vat_ref.shape[0]):
        rows = ckvn[c * MLA_BLOCK:(c + 1) * MLA_BLOCK, :]
        vat_ref[c] = lax.dot_general(wuvt_ref[...], rows, NT_DIMS, preferred_element_type=F32).astype(BF16)

    qb_ref[...] = proj(C_BQ, 256).astype(BF16)
    kb_ref[...] = proj(C_BK, 256).astype(BF16)
    vb_ref[...] = proj(C_BV, 256).astype(BF16)

    for t in range(2):
        lo = t * LANES
        qc_ref[:, lo:lo + LANES] = (proj(C_CQ + lo, LANES) * cos_c + proj(C_CQR + lo, LANES) * sin_c).astype(BF16)
    kc_ref[...] = (proj(C_CK, LANES) * cos_c + proj(C_CKR, LANES) * sin_c).astype(BF16)
    vc_ref[...] = proj(C_CV, LANES).astype(BF16)


def _projections(x2d, lp, qn, kvn, tabs, seq):
    t = x2d.shape[0]
    tile = min(PROJ_TILE, seq)
    n_seq = seq // tile
    sub = tile // MLA_BLOCK
    full = lambda a: pl.BlockSpec(a.shape, lambda i: (0,) * a.ndim)
    row = lambda w: pl.BlockSpec((tile, w), lambda i: (i, 0))
    sds = lambda w: jax.ShapeDtypeStruct((t, w), BF16)
    n_v = MLA_HEADS * MLA_V
    vat_spec = pl.BlockSpec((sub, n_v, MLA_BLOCK), lambda i: (i, 0, 0))
    vat_sds = jax.ShapeDtypeStruct((t // MLA_BLOCK, n_v, MLA_BLOCK), BF16)
    widths = (256, 256, 256, 256, 128, 128)
    return pl.pallas_call(
        _proj_kernel,
        grid=(t // tile,),
        in_specs=[row(D_MODEL), full(lp['w_in']), full(qn), full(lp['w_uq']), full(kvn), full(lp['w_uk']),
                  full(lp['w_uv_t']), pl.BlockSpec((6, tile, LANES), lambda i: (0, i % n_seq, 0))],
        out_specs=[row(1024), row(1024), vat_spec] + [row(w) for w in widths],
        out_shape=[sds(1024), sds(1024), vat_sds] + [sds(w) for w in widths],
        compiler_params=_params("parallel"),
        name="projections",
    )(x2d, lp['w_in'], qn, lp['w_uq'], kvn, lp['w_uk'], lp['w_uv_t'], tabs)


def _mla_kernel(q_ref, k_ref, vt_ref, o_ref):
    i = pl.program_id(2)
    blk = q_ref.shape[0]
    kchunk = lax.broadcasted_iota(jnp.int32, (blk, blk), 0) // CHUNK
    qchunk = lax.broadcasted_iota(jnp.int32, (blk, blk), 1) // CHUNK
    diag_visible = kchunk <= qchunk
    qs = [q_ref[:, r * HEAD_PAD:(r + 1) * HEAD_PAD] for r in range(2)]
    ones_rows = (lax.broadcasted_iota(jnp.int32, (MLA_SUM_ROWS, blk), 0) == 0).astype(BF16)

    def steps(blocks, carry, last_is_diagonal):
        scores = [[lax.dot_general(k_ref[pl.ds(pl.multiple_of(j * blk, blk), blk), r * HEAD_PAD:(r + 1) * HEAD_PAD],
                                   qs[r], NT_DIMS, preferred_element_type=F32) for r in range(2)]
                  for j in blocks]
        for n, j in enumerate(blocks):
            out = []
            for r in range(2):
                m, acc = carry[r]
                s = scores[n][r]
                if last_is_diagonal and n == len(blocks) - 1:
                    s = jnp.where(diag_visible, s, NEG_INF)
                m_new = jnp.maximum(m, jnp.max(s, axis=0, keepdims=True))
                p = jnp.exp((s - m_new).astype(BF16))
                a = jnp.exp(m - m_new)
                v_aug = jnp.concatenate([vt_ref[j, r * MLA_V:(r + 1) * MLA_V, :], ones_rows], axis=0)
                out.append((m_new, a * acc + jnp.dot(v_aug, p, preferred_element_type=F32)))
            carry = tuple(out)
        return carry

    one = (jnp.full((1, blk), NEG_INF, F32), jnp.zeros((MLA_V + MLA_SUM_ROWS, blk), F32))
    carry = lax.fori_loop(0, i // 2,lambda t, c: steps([2 * t, 2 * t + 1], c, False), (one, one))
    carry = lax.cond(i % 2 == 1, lambda c: steps([i - 1, i], c, True), lambda c: steps([i], c, True), carry)
    o_t = jnp.concatenate([acc[:MLA_V] / acc[MLA_V:MLA_V + 1] for _, acc in carry], axis=0)
    o_ref[...] = o_t.T.astype(BF16)


def _mla_attention(qa, ka, vat, batch, seq):
    blk = MLA_BLOCK
    nq = seq // blk
    pairs = MLA_HEADS // 2
    return pl.pallas_call(
        _mla_kernel,
        grid=(batch, pairs, nq),
        in_specs=[pl.BlockSpec((blk, 2 * HEAD_PAD), lambda b, p, i: (b * nq + i, p)),
                  pl.BlockSpec((seq, 2 * HEAD_PAD), lambda b, p, i: (b, p)),
                  pl.BlockSpec((nq, 2 * MLA_V, blk), lambda b, p, i: (b, p, 0))],
        out_specs=pl.BlockSpec((blk, 2 * MLA_V), lambda b, p, i: (b * nq + i, p)),
        out_shape=jax.ShapeDtypeStruct((batch * seq, MLA_HEADS * MLA_V), BF16),
        compiler_params=_params("parallel", "parallel", "arbitrary"),
        name="mla_attention",
    )(qa, ka, vat)


def _rel_bias_kernel(ext_ref, o_ref):
    blk = o_ref.shape[1]
    base = jnp.broadcast_to(ext_ref[0], (blk, REL_EXT))
    tile = pltpu.roll(base, REL_EXT - blk, 1, stride=1, stride_axis=0)[:, :3 * blk]
    qchunk = lax.broadcasted_iota(jnp.int32, (blk, 3 * blk), 0) // CHUNK + REL_BACK_CHUNKS
    kchunk = lax.broadcasted_iota(jnp.int32, (blk, 3 * blk), 1) // CHUNK
    visible = (kchunk >= qchunk - REL_BACK_CHUNKS) & (kchunk <= qchunk)
    o_ref[0] = jnp.where(visible, tile, NEG_INF)


def _rel_bias_tile(rel_ext):
    blk = ATT_BLOCK
    return pl.pallas_call(
        _rel_bias_kernel,
        grid=(REL_HEADS,),
        in_specs=[pl.BlockSpec((1, 1, REL_EXT), lambda h: (h, 0, 0))],
        out_specs=pl.BlockSpec((1, blk, 3 * blk), lambda h: (h, 0, 0)),
        out_shape=jax.ShapeDtypeStruct((REL_HEADS, blk, 3 * blk), F32),
        compiler_params=_params("parallel"),
        name="rel_bias_tile",
    )(rel_ext)


def _rel_kernel(q_ref, k0_ref, k1_ref, k2_ref, v0_ref, v1_ref, v2_ref, bias_ref, o_ref):
    i = pl.program_id(1)
    blk = q_ref.shape[0]
    kwin = jnp.concatenate([k0_ref[...], k1_ref[...], k2_ref[...]], axis=0)
    vwin = jnp.concatenate([v0_ref[...], v1_ref[...], v2_ref[...]], axis=0)
    col = lax.broadcasted_iota(jnp.int32, (1, 3 * blk), 1)
    in_seq = col >= (2 - i) * blk
    left = lax.broadcasted_iota(jnp.int32, (1, LANES), 1) < REL_DIM
    heads = [(p, r) for p in range(REL_HEADS // 2) for r in range(2)]
    scores = []
    for p, r in heads:
        q = q_ref[:, p * LANES:(p + 1) * LANES]
        qm = jnp.where(left if r == 0 else jnp.logical_not(left), q, jnp.zeros_like(q))
        scores.append(lax.dot_general(qm, kwin[:, p * LANES:(p + 1) * LANES], NT_DIMS, preferred_element_type=F32))
    outs = []
    for n, (p, r) in enumerate(heads):
        s = jnp.where(in_seq, scores[n] + bias_ref[n], NEG_INF)
        m = jnp.max(s, axis=-1, keepdims=True)
        e = jnp.exp(s - m)
        l = jnp.sum(e, axis=-1, keepdims=True)
        outs.append(jnp.dot(e.astype(BF16), vwin[:, p * LANES:(p + 1) * LANES], preferred_element_type=F32) / l)
    for p in range(REL_HEADS // 2):
        o_ref[:, p * LANES:(p + 1) * LANES] = jnp.where(left, outs[2 * p], outs[2 * p + 1]).astype(BF16)


def _rel_attention(qb, kb, vb, bias, batch, seq):
    blk = ATT_BLOCK
    nq = seq // blk
    width = REL_HEADS * REL_DIM
    back = lambda d: (lambda b, i: (b * nq + jnp.maximum(i - d, 0), 0))
    tile = lambda d: pl.BlockSpec((blk, width), back(d))
    return pl.pallas_call(
        _rel_kernel,
        grid=(batch, nq),
        in_specs=[tile(0), tile(2), tile(1), tile(0), tile(2), tile(1), tile(0),
                  pl.BlockSpec((REL_HEADS, blk, 3 * blk), lambda b, i: (0, 0, 0))],
        out_specs=tile(0),
        out_shape=jax.ShapeDtypeStruct((batch * seq, width), BF16),
        compiler_params=_params("parallel", "arbitrary"),
        name="rel_attention",
    )(qb, kb, kb, kb, vb, vb, vb, bias)


def _swa_kernel(sink_ref, q_ref, kp_ref, kc_ref, vp_ref, vc_ref, o_ref):
    i = pl.program_id(1)
    blk = q_ref.shape[0]
    back = SWA_BACK_CHUNKS * CHUNK
    kwin = jnp.concatenate([kp_ref[blk - back:, :], kc_ref[...]], axis=0)
    vwin = jnp.concatenate([vp_ref[blk - back:, :], vc_ref[...]], axis=0)
    qchunk = lax.broadcasted_iota(jnp.int32, (blk, blk + back), 0) // CHUNK
    col = lax.broadcasted_iota(jnp.int32, (blk, blk + back), 1)
    kchunk = col // CHUNK
    visible = (kchunk >= qchunk) & (kchunk <= qchunk + SWA_BACK_CHUNKS) & ((col >= back) | (i > 0))
    left = lax.broadcasted_iota(jnp.int32, (1, LANES), 1) < SWA_DIM
    heads = [(t, r) for t in range(2) for r in range(2)]
    scores = []
    for t, r in heads:
        q = q_ref[:, t * LANES:(t + 1) * LANES]
        qm = jnp.where(left if r == 0 else jnp.logical_not(left), q, jnp.zeros_like(q))
        scores.append(lax.dot_general(qm, kwin, NT_DIMS, preferred_element_type=F32))
    outs = []
    for n, (t, r) in enumerate(heads):
        s = jnp.where(visible, scores[n], NEG_INF)
        sink = sink_ref[2 * r + t]
        m = jnp.maximum(jnp.max(s, axis=-1, keepdims=True), sink)
        e = jnp.exp(s - m)
        l = jnp.sum(e, axis=-1, keepdims=True) + jnp.exp(sink - m)
        outs.append(jnp.dot(e.astype(BF16), vwin, preferred_element_type=F32) / l)
    for t in range(2):
        o_ref[:, t * LANES:(t + 1) * LANES] = jnp.where(left, outs[2 * t], outs[2 * t + 1]).astype(BF16)


def _swa_attention(qc, kc, vc, sinks, batch, seq):
    blk = ATT_BLOCK
    nq = seq // blk
    cur = lambda b, i: (b * nq + i, 0)
    prev = lambda b, i: (b * nq + jnp.maximum(i - 1, 0), 0)
    width = SWA_Q_HEADS * SWA_DIM
    return pl.pallas_call(
        _swa_kernel,
        grid=(batch, nq),
        in_specs=[pl.BlockSpec(memory_space=pltpu.SMEM),
                  pl.BlockSpec((blk, width), cur),
                  pl.BlockSpec((blk, LANES), prev), pl.BlockSpec((blk, LANES), cur),
                  pl.BlockSpec((blk, LANES), prev), pl.BlockSpec((blk, LANES), cur)],
        out_specs=pl.BlockSpec((blk, width), cur),
        out_shape=jax.ShapeDtypeStruct((batch * seq, width), BF16),
        compiler_params=_params("parallel", "arbitrary"),
        name="swa_attention",
    )(sinks, qc, kc, kc, vc, vc)


def _oddeven_merge_sort_pairs(n):
    pairs = []

    def merge(lo, hi, r):
        step = r * 2
        if step < hi - lo:
            merge(lo, hi, step)
            merge(lo + r, hi, step)
            pairs.extend((i, i + r) for i in range(lo + r, hi - r, step))
        else:
            pairs.append((lo, lo + r))

    def sort(lo, hi):
        if hi - lo >= 1:
            mid = lo + (hi - lo) // 2
            sort(lo, mid)
            sort(mid + 1, hi)
            merge(lo, hi, 1)

    sort(0, n - 1)
    return pairs


def _bitonic_merge_pairs(n):
    pairs = []
    s = n // 2
    while s >= 1:
        pairs.extend((i, i + s) for i in range(n) if (i // s) % 2 == 0)
        s //= 2
    return pairs


SORT16 = _oddeven_merge_sort_pairs(PEER_TOPK)
BITONIC16 = _bitonic_merge_pairs(PEER_TOPK)
STAIRCASE = [(a, b) for a in range(1, PEER_TOPK) for b in range(PEER_TOPK) if (a + 1) * (b + 1) <= PEER_TOPK]


def _compare_exchange(vals, pairs):
    vals = list(vals)
    for a, b in pairs:
        hi = jnp.maximum(vals[a], vals[b])
        lo = jnp.minimum(vals[a], vals[b])
        vals[a], vals[b] = hi, lo
    return vals


def _merge_top16(xs, ys):
    return _compare_exchange([jnp.maximum(xs[i], ys[PEER_TOPK - 1 - i]) for i in range(PEER_TOPK)], BITONIC16)


def _sorted_top16(s):
    slabs = _compare_exchange([s[SUBLANES * v:SUBLANES * (v + 1), :] for v in range(N_KEYS // SUBLANES)], SORT16)
    for shift in (4, 2, 1):
        slabs = _merge_top16(slabs, [pltpu.roll(x, shift, 0) for x in slabs])
    return slabs


def _count_prefix(test, vals):
    cands = list(vals[:PEER_TOPK - 1])
    count = None
    step = PEER_TOPK // 2
    while step >= 1:
        c = test(cands[step - 1])
        inc = jnp.where(c, float(step), 0.0)
        count = inc if count is None else count + inc
        cands = [jnp.where(c, cands[k + step], cands[k]) for k in range(step - 1)]
        step //= 2
    return count + jnp.where(test(vals[PEER_TOPK - 1]), 1.0, 0.0)


def _take_top(s, rows, n_rows, on_pick):
    m = jnp.max(s, axis=0, keepdims=True)
    idx = jnp.min(jnp.where(s == m, rows, float(n_rows)), axis=0, keepdims=True)
    hit = rows == idx
    on_pick(m, hit)
    return jnp.where(hit, -jnp.inf, s), hit


def _route_kernel(ya_ref, yb_ref, yc_ref, x_ref, wout_ref, g_ref, b_ref, wq_ref, keys_ref,
                  x1_ref, x1b_ref, rb_ref, e2_ref, nb_ref, e1_ref,
                  q_scr, sc_scr, rank_scr, top_scr, cand_scr, w_scr):
    tt = x_ref.shape[0]
    n_a, n_b = ya_ref.shape[1], yb_ref.shape[1]
    mix = (jnp.dot(ya_ref[...], wout_ref[0:n_a, :], preferred_element_type=F32)
           + jnp.dot(yb_ref[...], wout_ref[n_a:n_a + n_b, :], preferred_element_type=F32)
           + jnp.dot(yc_ref[...], wout_ref[n_a + n_b:, :], preferred_element_type=F32))
    x1 = _layer_norm(DEEPNORM_ALPHA * x_ref[...] + mix, g_ref[...], b_ref[...])
    x1_ref[...] = x1
    x1b = x1.astype(BF16)
    x1b_ref[...] = x1b
    q = jnp.dot(x1b, wq_ref[...], preferred_element_type=F32).astype(BF16)
    for h in range(PEER_HEADS):
        q_scr[h] = q[:, h * LANES:(h + 1) * LANES]

    def emit(h, rank2, n_sel, e1, e2):
        rb_ref[h] = rank2.astype(BF16)
        e2_ref[h] = e2.astype(BF16)
        nb_ref[h] = n_sel
        e1_ref[h] = e1

    n_slab = N_KEYS // SUBLANES

    def fast_head(h, bad):
        s1 = lax.dot_general(keys_ref[2 * h], q_scr[h], NT_DIMS, preferred_element_type=F32)
        s2 = lax.dot_general(keys_ref[2 * h + 1], q_scr[h], NT_DIMS, preferred_element_type=F32)
        sc_scr[2 * h] = s1
        sc_scr[2 * h + 1] = s2
        t1 = _sorted_top16(s1)
        t2 = _sorted_top16(s2)
        neg = jnp.full_like(t1[0], -jnp.inf)
        rest = [t1[a] + t2[b] for a, b in STAIRCASE]
        rest += [neg] * (3 * PEER_TOPK - len(rest))
        groups = [_compare_exchange(rest[g * PEER_TOPK:(g + 1) * PEER_TOPK], SORT16) for g in range(3)]
        row0 = [t1[0] + t2[b] for b in range(PEER_TOPK)]
        ctop = _merge_top16(_merge_top16(row0, groups[0]), _merge_top16(groups[1], groups[2]))
        tau = ctop[PEER_TOPK - 1]
        z = jnp.exp(ctop[0] - ctop[0])
        for r in range(1, PEER_TOPK):
            z = z + jnp.exp(ctop[r] - ctop[0])
        inv_z = 1.0 / z
        in_top = jnp.zeros_like(tau)
        sel_mass = jnp.zeros_like(tau)
        ranks, counts, e1s, e2s = [], [], [], []
        for v in range(n_slab):
            a1 = s1[SUBLANES * v:SUBLANES * (v + 1), :]
            a2 = s2[SUBLANES * v:SUBLANES * (v + 1), :]
            rank2 = _count_prefix(lambda top: top > a2, t2)
            n_sel = _count_prefix(lambda top: a1 + top >= tau, t2)
            in_top = in_top + jnp.minimum(float(PEER_TOPK) - rank2, 1.0)
            sel_mass = sel_mass + n_sel
            ranks.append(rank2)
            counts.append(n_sel)
            e1s.append(jnp.exp(a1 - t1[0]) * inv_z)
            e2s.append(jnp.exp(a2 - t2[0]))
        emit(h, jnp.concatenate(ranks, axis=0), jnp.concatenate(counts, axis=0),
             jnp.concatenate(e1s, axis=0), jnp.concatenate(e2s, axis=0))
        in_top = jnp.sum(in_top, axis=0, keepdims=True)
        sel_mass = jnp.sum(sel_mass, axis=0, keepdims=True)
        tied = (in_top != float(PEER_TOPK)) | (sel_mass != float(PEER_TOPK))
        return jnp.maximum(bad, jnp.where(tied, 1.0, 0.0))

    bad = lax.fori_loop(0, PEER_HEADS, fast_head, jnp.zeros((1, tt), F32))

    @pl.when(jnp.max(bad) > 0.0)
    def _exact_with_ties():
        key_rows = lax.broadcasted_iota(jnp.int32, (N_KEYS, tt), 0).astype(F32)

        def half_body(hp, _):
            s0 = sc_scr[hp]

            def pick(r, carry):
                s, rank = carry

                def on_pick(m, hit):
                    top_scr[hp, pl.ds(r, 1), :] = m

                s, hit = _take_top(s, key_rows, N_KEYS, on_pick)
                return s, jnp.where(hit, r.astype(F32), rank)

            _, rank = lax.fori_loop(0, PEER_TOPK, pick, (s0, jnp.full((N_KEYS, tt), float(PEER_TOPK), F32)))
            rank_scr[hp] = rank
            return 0

        lax.fori_loop(0, 2 * PEER_HEADS, half_body, 0)

        n_cand = PEER_TOPK * PEER_TOPK
        cand_rows = lax.broadcasted_iota(jnp.int32, (n_cand, tt), 0).astype(F32)

        def head_body(h, _):
            t1 = top_scr[2 * h]
            t2 = top_scr[2 * h + 1]
            e1t = jnp.exp(t1 - t1[0:1])
            e2t = jnp.exp(t2 - t2[0:1])
            for a in range(PEER_TOPK):
                cand_scr[a * PEER_TOPK:(a + 1) * PEER_TOPK, :] = t1[a:a + 1] + t2
                w_scr[a * PEER_TOPK:(a + 1) * PEER_TOPK, :] = e1t[a:a + 1] * e2t

            def pick(r, carry):
                c, sel = carry
                c, hit = _take_top(c, cand_rows, n_cand, lambda m, hit: None)
                return c, jnp.where(hit, 1.0, sel)

            _, sel = lax.fori_loop(0, PEER_TOPK, pick, (cand_scr[...], jnp.zeros((n_cand, tt), F32)))
            z = jnp.sum(sel * w_scr[...], axis=0, keepdims=True)
            rank1 = rank_scr[2 * h]
            n_sel = jnp.zeros((N_KEYS, tt), F32)
            for a in range(PEER_TOPK):
                cnt = jnp.sum(sel[a * PEER_TOPK:(a + 1) * PEER_TOPK, :], axis=0, keepdims=True)
                n_sel = jnp.where(rank1 == float(a), cnt, n_sel)
            emit(h, rank_scr[2 * h + 1], n_sel, jnp.exp(sc_scr[2 * h] - t1[0:1]) / z,
                 jnp.exp(sc_scr[2 * h + 1] - t2[0:1]))
            return 0

        lax.fori_loop(0, PEER_HEADS, head_body, 0)


def _route(ya, yb, yc, x2d, lp, g, b):
    t = x2d.shape[0]
    tt = min(ROUTE_TILE, t)
    full = lambda a: pl.BlockSpec(a.shape, lambda i: (0,) * a.ndim)
    row = lambda w: pl.BlockSpec((tt, w), lambda i: (i, 0))
    route_spec = pl.BlockSpec((PEER_HEADS, N_KEYS, tt), lambda i: (0, 0, i))
    route_shape = lambda dt: jax.ShapeDtypeStruct((PEER_HEADS, N_KEYS, t), dt)
    n_cand = PEER_TOPK * PEER_TOPK
    return pl.pallas_call(
        _route_kernel,
        grid=(t // tt,),
        in_specs=[row(ya.shape[1]), row(yb.shape[1]), row(yc.shape[1]), row(D_MODEL), full(lp['w_out']),
                  full(g), full(b), full(lp['wq']), full(lp['keys'])],
        out_specs=[row(D_MODEL), row(D_MODEL), route_spec, route_spec, route_spec, route_spec],
        out_shape=[jax.ShapeDtypeStruct((t, D_MODEL), F32), jax.ShapeDtypeStruct((t, D_MODEL), BF16),
                   route_shape(BF16), route_shape(BF16), route_shape(F32), route_shape(F32)],
        scratch_shapes=[pltpu.VMEM((PEER_HEADS, tt, LANES), BF16),
                        pltpu.VMEM((2 * PEER_HEADS, N_KEYS, tt), F32),
                        pltpu.VMEM((2 * PEER_HEADS, N_KEYS, tt), F32),
                        pltpu.VMEM((2 * PEER_HEADS, PEER_TOPK, tt), F32),
                        pltpu.VMEM((n_cand, tt), F32),
                        pltpu.VMEM((n_cand, tt), F32)],
        compiler_params=_params("parallel"),
        name="route",
    )(ya, yb, yc, x2d, lp['w_out'], g, b, lp['wq'], lp['keys'])


def _peer_kernel(x1b_ref, u_ref, vt_ref, rb_ref, e2_ref, nb_ref, e1_ref, x1_ref, g_ref, b_ref,
                 o_ref, acc_ref, act_ref):
    ei = pl.program_id(1)
    tt = x1b_ref.shape[0]
    n_sub = u_ref.shape[0] // N_KEYS

    @pl.when(ei == 0)
    def _():
        acc_ref[...] = jnp.zeros_like(acc_ref)

    def gate_rows(ii):
        i_row = ei * n_sub + ii
        gate = jnp.zeros((N_KEYS, tt), BF16)
        for h in range(PEER_HEADS):
            n_sel = jnp.broadcast_to(nb_ref[h, pl.ds(i_row, 1), :], (N_KEYS, tt)).astype(BF16)
            e1 = jnp.broadcast_to(e1_ref[h, pl.ds(i_row, 1), :], (N_KEYS, tt)).astype(BF16)
            gate = gate + jnp.where(rb_ref[h] < n_sel, e2_ref[h], jnp.zeros((), BF16)) * e1
        return gate

    per = n_sub // PEER_DOT_CHUNKS
    chunk_rows = per * N_KEYS
    gates = {}
    hids = []
    for c in range(PEER_DOT_CHUNKS):
        lhs = u_ref[c * chunk_rows:(c + 1) * chunk_rows, :]
        if c >= PEER_PACE_LAG:
            for ii in range((c - PEER_PACE_LAG) * per, (c - PEER_PACE_LAG + 1) * per):
                gates[ii] = gate_rows(ii)
            tile = gates[(c - PEER_PACE_LAG + 1) * per - 1][0:2 * SUBLANES, 0:LANES]
            zero = pltpu.bitcast((pltpu.bitcast(tile, jnp.uint32) >> 16) >> 16, tile.dtype)
            head = lhs[0:2 * SUBLANES, :] + jnp.concatenate([zero] * (D_MODEL // LANES), axis=1)
            lhs = jnp.concatenate([head, lhs[2 * SUBLANES:, :]], axis=0)
        hids.append(lax.dot_general(lhs, x1b_ref[...], NT_DIMS, preferred_element_type=F32))
    for ii in range((PEER_DOT_CHUNKS - PEER_PACE_LAG) * per, n_sub):
        gates[ii] = gate_rows(ii)
    for ii in range(n_sub):
        hid = hids[ii // per][(ii % per) * N_KEYS:(ii % per + 1) * N_KEYS, :].astype(BF16)
        gelu = (hid * 0.5) * (1.0 + lax.erf(hid * RSQRT2))
        act_ref[ii * N_KEYS:(ii + 1) * N_KEYS, :] = gelu * gates[ii]
    acc_ref[...] += jnp.dot(vt_ref[...], act_ref[...], preferred_element_type=F32)

    @pl.when(ei == pl.num_programs(1) - 1)
    def _():
        ffn = acc_ref[...].T
        o_ref[...] = _layer_norm(DEEPNORM_ALPHA * x1_ref[...] + ffn, g_ref[...], b_ref[...])


def _peer(x1, x1b, rb, e2, nb, e1, lp, g, b):
    t = x1.shape[0]
    tt = min(PEER_TOKEN_TILE, t)
    eb = PEER_EXPERT_BLOCK
    full = lambda a: pl.BlockSpec(a.shape, lambda ti, ei: (0,) * a.ndim)
    route_spec = pl.BlockSpec((PEER_HEADS, N_KEYS, tt), lambda ti, ei: (0, 0, ti))
    return pl.pallas_call(
        _peer_kernel,
        grid=(t // tt, N_EXPERTS // eb),
        in_specs=[pl.BlockSpec((tt, D_MODEL), lambda ti, ei: (ti, 0)),
                  pl.BlockSpec((eb, D_MODEL), lambda ti, ei: (ei, 0)),
                  pl.BlockSpec((D_MODEL, eb), lambda ti, ei: (0, ei)),
                  route_spec, route_spec, route_spec, route_spec,
                  pl.BlockSpec((tt, D_MODEL), lambda ti, ei: (ti, 0)), full(g), full(b)],
        out_specs=pl.BlockSpec((tt, D_MODEL), lambda ti, ei: (ti, 0)),
        out_shape=jax.ShapeDtypeStruct((t, D_MODEL), F32),
        scratch_shapes=[pltpu.VMEM((D_MODEL, tt), F32), pltpu.VMEM((eb, tt), BF16)],
        compiler_params=_params("parallel", "arbitrary"),
        name="peer_experts",
    )(x1b, lp['u'], lp['v_t'], rb, e2, nb, e1, x1, g, b)


def kernel(x, w_in, mla_q_norm, mla_w_uq, mla_kv_norm, mla_w_ukv, rel_bias, swa_sinks, w_out, ln1_g, ln1_b,
           peer_wq, peer_keys, peer_u, peer_v, ln2_g, ln2_b):
    batch, seq, d = x.shape
    assert d == D_MODEL and seq % PROJ_TILE == 0 and PROJ_TILE % MLA_BLOCK == 0 and MLA_BLOCK % ATT_BLOCK == 0
    tabs = _make_tables(seq)
    h = x.reshape(batch * seq, d)
    for l in range(w_in.shape[0]):
        lp = _prep_layer(w_in[l], mla_w_uq[l], mla_w_ukv[l], rel_bias[l], w_out[l], peer_wq[l], peer_keys[l],
                         peer_u[l], peer_v[l])
        row = lambda a: a[l].reshape(1, -1)
        qa, ka, vat, qb, kb, vb, qc, kc, vc = _projections(h, lp, row(mla_q_norm), row(mla_kv_norm), tabs, seq)
        ya = _mla_attention(qa, ka, vat, batch, seq)
        yb = _rel_attention(qb, kb, vb, _rel_bias_tile(lp['rel_ext']), batch, seq)
        yc = _swa_attention(qc, kc, vc, swa_sinks[l], batch, seq)
        x1, x1b, rb, e2, nb, e1 = _route(ya, yb, yc, h, lp, row(ln1_g), row(ln1_b))
        h = _peer(x1, x1b, rb, e2, nb, e1, lp, row(ln2_g), row(ln2_b))
    return h.reshape(batch, seq, d)
```

```python
import numpy as np
import jax
import jax.numpy as jnp
from jax import lax
from jax.experimental import pallas as pl
from jax.experimental.pallas import tpu as pltpu

F32 = jnp.float32
BF16 = jnp.bfloat16

D_MODEL = 1024
DEPTH = 2
CHUNK = 64
ROPE_THETA = 10000.0
LN_EPS = 1e-5
RMS_EPS = 1e-6
NEG_INF = -1e30
DEEPNORM_ALPHA = (2.0 * DEPTH) ** 0.25

MLA_HEADS = 8
MLA_Q_RANK = 384
MLA_KV_RANK = 256
MLA_NOPE = 64
MLA_ROPE = 32
MLA_V = 64
MLA_DK = MLA_NOPE + MLA_ROPE

REL_HEADS = 4
REL_DIM = 64
REL_BACK_CHUNKS = 8
MAX_REL_DIST = 256

SWA_Q_HEADS = 4
SWA_KV_HEADS = 2
SWA_DIM = 64
SWA_BACK_CHUNKS = 2

IN_SPLITS = [MLA_Q_RANK, MLA_KV_RANK, MLA_ROPE, 256, 256, 256, 256, 128, 128]
IN_SPLIT_POINTS = [int(c) for c in np.cumsum(IN_SPLITS)[:-1]]

PEER_HEADS = 8
N_KEYS = 128
N_EXPERTS = N_KEYS * N_KEYS
PEER_HALF = 64
PEER_TOPK = 16

LANES = 128
SUBLANES = 8
HEAD_PAD = 128
SWA_HEAD_ORDER = (0, 2, 1, 3)

C_AQ, C_AKV, C_KR, C_KRR = 0, 384, 640, 768
C_BQ, C_BK, C_BV = 896, 1152, 1408
C_CQ, C_CQR, C_CK, C_CKR, C_CV = 1664, 1920, 2176, 2304, 2432

PROJ_TILE = 512
ATT_BLOCK = 256
MLA_BLOCK = 512
MLA_SUM_ROWS = 16
ROUTE_TILE = 512
PEER_TOKEN_TILE = 512
PEER_EXPERT_BLOCK = 2048
PEER_DOT_CHUNKS = 8
PEER_PACE_LAG = 1
REL_EXT = 1024
VMEM_LIMIT = 56 * 1024 * 1024

NT_DIMS = (((1,), (1,)), ((), ()))
RSQRT2 = float(np.sqrt(0.5))


def _rot_half_cols(w, head_dim):
    k, c = w.shape
    w3 = w.reshape(k, c // head_dim, head_dim)
    half = head_dim // 2
    return jnp.concatenate([-w3[..., half:], w3[..., :half]], axis=-1).reshape(k, c)


def _place(w, lo, width=LANES):
    return jnp.pad(w, ((0, 0), (lo, width - lo - w.shape[1])))


def _rope_tables(seq, dim):
    inv = ROPE_THETA ** (-jnp.arange(0, dim, 2, dtype=F32) / dim)
    ang = jnp.arange(seq, dtype=F32)[:, None] * inv[None, :]
    return jnp.cos(ang), jnp.sin(ang)


def _make_tables(seq):
    c16, s16 = _rope_tables(seq, MLA_ROPE)
    c32, s32 = _rope_tables(seq, SWA_DIM)
    one = jnp.ones((seq, MLA_NOPE), F32)
    z64 = jnp.zeros((seq, MLA_NOPE), F32)
    z32 = jnp.zeros((seq, LANES - MLA_DK), F32)
    scale_a = MLA_DK ** -0.5
    cos_a = jnp.concatenate([one, c16, c16, z32], axis=1) * scale_a
    sin_a = jnp.concatenate([z64, s16, s16, z32], axis=1) * scale_a
    cos_k = jnp.concatenate([z64, c16, c16, z32], axis=1)
    sin_k = jnp.concatenate([z64, s16, s16, z32], axis=1)
    cos_c = jnp.concatenate([c32, c32, c32, c32], axis=1)
    sin_c = jnp.concatenate([s32, s32, s32, s32], axis=1)
    return jnp.stack([cos_a, sin_a, cos_k, sin_k, cos_c, sin_c])


def _prep_layer(w_in, w_uq, w_ukv, rel_bias, w_out, peer_wq, peer_keys, peer_u, peer_v):
    a_q, a_kv, a_kr, b_q, b_k, b_v, c_q, c_k, c_v = jnp.split(w_in, IN_SPLIT_POINTS, axis=1)
    c_q = c_q.reshape(D_MODEL, SWA_Q_HEADS, SWA_DIM)[:, SWA_HEAD_ORDER, :].reshape(D_MODEL, -1)
    att_scale = REL_DIM ** -0.5
    w_in_ext = jnp.concatenate([
        a_q, a_kv, _place(a_kr, MLA_NOPE), _place(_rot_half_cols(a_kr, MLA_ROPE), MLA_NOPE),
        b_q * att_scale, b_k, b_v,
        c_q * att_scale, _rot_half_cols(c_q, SWA_DIM) * att_scale, c_k, _rot_half_cols(c_k, SWA_DIM), c_v,
    ], axis=1).astype(BF16)

    uq = w_uq.reshape(MLA_Q_RANK, MLA_HEADS, MLA_DK)
    uq_pad = jnp.pad(uq, ((0, 0), (0, 0), (0, HEAD_PAD - MLA_DK))).reshape(MLA_Q_RANK, -1)
    uq_rope = uq[..., MLA_NOPE:]
    half = MLA_ROPE // 2
    uq_rot = jnp.concatenate([-uq_rope[..., half:], uq_rope[..., :half]], axis=-1)
    uq_rot = jnp.pad(uq_rot, ((0, 0), (0, 0), (MLA_NOPE, HEAD_PAD - MLA_DK))).reshape(MLA_Q_RANK, -1)
    w_uq_ext = jnp.concatenate([uq_pad, uq_rot], axis=1).astype(BF16)

    ukv = w_ukv.reshape(MLA_KV_RANK, MLA_HEADS, MLA_NOPE + MLA_V)
    w_uk = jnp.pad(ukv[..., :MLA_NOPE], ((0, 0), (0, 0), (0, HEAD_PAD - MLA_NOPE))).reshape(MLA_KV_RANK, -1).astype(BF16)
    w_uv_t = ukv[..., MLA_NOPE:].reshape(MLA_KV_RANK, -1).T.astype(BF16)

    m = jnp.arange(REL_EXT)
    ext_idx = jnp.clip(3 * ATT_BLOCK - m, -(CHUNK - 1), MAX_REL_DIST) + (CHUNK - 1)
    rel_ext = rel_bias[:, ext_idx].astype(F32).reshape(REL_HEADS, 1, REL_EXT)

    n_a = MLA_HEADS * MLA_V
    n_b = REL_HEADS * REL_DIM
    w_out_c = w_out[n_a + n_b:].reshape(SWA_Q_HEADS, SWA_DIM, D_MODEL)[SWA_HEAD_ORDER, :, :].reshape(-1, D_MODEL)
    w_out_ext = jnp.concatenate([w_out[:n_a + n_b], w_out_c], axis=0).astype(BF16)

    keys = peer_keys.reshape(PEER_HEADS * 2, N_KEYS, PEER_HALF)
    keys_lo = jnp.pad(keys, ((0, 0), (0, 0), (0, PEER_HALF)))
    keys_hi = jnp.pad(keys, ((0, 0), (0, 0), (PEER_HALF, 0)))
    is_hi = (jnp.arange(PEER_HEADS * 2) % 2 == 1)[:, None, None]
    keys_ext = jnp.where(is_hi, keys_hi, keys_lo).astype(BF16)

    return dict(w_in=w_in_ext, w_uq=w_uq_ext, w_uk=w_uk, w_uv_t=w_uv_t, rel_ext=rel_ext, w_out=w_out_ext,
                wq=peer_wq.astype(BF16), keys=keys_ext, u=peer_u.astype(BF16), v_t=peer_v.T.astype(BF16))


def _params(*sem):
    return pltpu.CompilerParams(dimension_semantics=sem, vmem_limit_bytes=VMEM_LIMIT)


def _layer_norm(z, g, b):
    mu = jnp.mean(z, axis=-1, keepdims=True)
    zc = z - mu
    var = jnp.mean(zc * zc, axis=-1, keepdims=True)
    return zc * lax.rsqrt(var + LN_EPS) * g + b


def _rms_norm(x, g):
    return x * lax.rsqrt(jnp.mean(x * x, axis=-1, keepdims=True) + RMS_EPS) * g


def _proj_kernel(x_ref, win_ref, qn_ref, wuq_ref, kvn_ref, wuk_ref, wuvt_ref, tab_ref,
                 qa_ref, ka_ref, vat_ref, qb_ref, kb_ref, vb_ref, qc_ref, kc_ref, vc_ref):
    xb = x_ref[...].astype(BF16)
    h_all = jnp.dot(xb, win_ref[...], preferred_element_type=F32)

    def proj(lo, width):
        return h_all[:, lo:lo + width]

    cos_a, sin_a, cos_k, sin_k, cos_c, sin_c = (tab_ref[i] for i in range(6))

    cqn = _rms_norm(proj(C_AQ, MLA_Q_RANK), qn_ref[...]).astype(BF16)
    rot0 = MLA_HEADS * HEAD_PAD
    q_all = jnp.dot(cqn, wuq_ref[...], preferred_element_type=F32)
    for h in range(MLA_HEADS):
        lo = h * HEAD_PAD
        qh = q_all[:, lo:lo + HEAD_PAD]
        qr = q_all[:, rot0 + lo:rot0 + lo + HEAD_PAD]
        qa_ref[:, lo:lo + HEAD_PAD] = (qh * cos_a + qr * sin_a).astype(BF16)

    ckvn = _rms_norm(proj(C_AKV, MLA_KV_RANK), kvn_ref[...]).astype(BF16)
    k_pe = proj(C_KR, LANES) * cos_k + proj(C_KRR, LANES) * sin_k
    k_all = jnp.dot(ckvn, wuk_ref[...], preferred_element_type=F32)
    for h in range(MLA_HEADS):
        lo = h * HEAD_PAD
        ka_ref[:, lo:lo + HEAD_PAD] = (k_all[:, lo:lo + HEAD_PAD] + k_pe).astype(BF16)
    for c in range(vat_ref.shape[0]):
        rows = ckvn[c * MLA_BLOCK:(c + 1) * MLA_BLOCK, :]
        vat_ref[c] = lax.dot_general(wuvt_ref[...], rows, NT_DIMS, preferred_element_type=F32).astype(BF16)

    qb_ref[...] = proj(C_BQ, 256).astype(BF16)
    kb_ref[...] = proj(C_BK, 256).astype(BF16)
    vb_ref[...] = proj(C_BV, 256).astype(BF16)

    for t in range(2):
        lo = t * LANES
        qc_ref[:, lo:lo + LANES] = (proj(C_CQ + lo, LANES) * cos_c + proj(C_CQR + lo, LANES) * sin_c).astype(BF16)
    kc_ref[...] = (proj(C_CK, LANES) * cos_c + proj(C_CKR, LANES) * sin_c).astype(BF16)
    vc_ref[...] = proj(C_CV, LANES).astype(BF16)


def _projections(x2d, lp, qn, kvn, tabs, seq):
    t = x2d.shape[0]
    tile = min(PROJ_TILE, seq)
    n_seq = seq // tile
    sub = tile // MLA_BLOCK
    full = lambda a: pl.BlockSpec(a.shape, lambda i: (0,) * a.ndim)
    row = lambda w: pl.BlockSpec((tile, w), lambda i: (i, 0))
    sds = lambda w: jax.ShapeDtypeStruct((t, w), BF16)
    n_v = MLA_HEADS * MLA_V
    vat_spec = pl.BlockSpec((sub, n_v, MLA_BLOCK), lambda i: (i, 0, 0))
    vat_sds = jax.ShapeDtypeStruct((t // MLA_BLOCK, n_v, MLA_BLOCK), BF16)
    widths = (256, 256, 256, 256, 128, 128)
    return pl.pallas_call(
        _proj_kernel,
        grid=(t // tile,),
        in_specs=[row(D_MODEL), full(lp['w_in']), full(qn), full(lp['w_uq']), full(kvn), full(lp['w_uk']),
                  full(lp['w_uv_t']), pl.BlockSpec((6, tile, LANES), lambda i: (0, i % n_seq, 0))],
        out_specs=[row(1024), row(1024), vat_spec] + [row(w) for w in widths],
        out_shape=[sds(1024), sds(1024), vat_sds] + [sds(w) for w in widths],
        compiler_params=_params("parallel"),
        name="projections",
    )(x2d, lp['w_in'], qn, lp['w_uq'], kvn, lp['w_uk'], lp['w_uv_t'], tabs)


def _mla_kernel(q_ref, k_ref, vt_ref, o_ref):
    i = pl.program_id(2)
    blk = q_ref.shape[0]
    kchunk = lax.broadcasted_iota(jnp.int32, (blk, blk), 0) // CHUNK
    qchunk = lax.broadcasted_iota(jnp.int32, (blk, blk), 1) // CHUNK
    diag_visible = kchunk <= qchunk
    qs = [q_ref[:, r * HEAD_PAD:(r + 1) * HEAD_PAD] for r in range(2)]
    ones_rows = (lax.broadcasted_iota(jnp.int32, (MLA_SUM_ROWS, blk), 0) == 0).astype(BF16)

    def steps(blocks, carry, last_is_diagonal):
        scores = [[lax.dot_general(k_ref[pl.ds(pl.multiple_of(j * blk, blk), blk), r * HEAD_PAD:(r + 1) * HEAD_PAD],
                                   qs[r], NT_DIMS, preferred_element_type=F32) for r in range(2)]
                  for j in blocks]
        for n, j in enumerate(blocks):
            out = []
            for r in range(2):
                m, acc = carry[r]
                s = scores[n][r]
                if last_is_diagonal and n == len(blocks) - 1:
                    s = jnp.where(diag_visible, s, NEG_INF)
                m_new = jnp.maximum(m, jnp.max(s, axis=0, keepdims=True))
                p = jnp.exp((s - m_new).astype(BF16))
                a = jnp.exp(m - m_new)
                v_aug = jnp.concatenate([vt_ref[j, r * MLA_V:(r + 1) * MLA_V, :], ones_rows], axis=0)
                out.append((m_new, a * acc + jnp.dot(v_aug, p, preferred_element_type=F32)))
            carry = tuple(out)
        return carry

    one = (jnp.full((1, blk), NEG_INF, F32), jnp.zeros((MLA_V + MLA_SUM_ROWS, blk), F32))
    carry = lax.fori_loop(0, i // 2,lambda t, c: steps([2 * t, 2 * t + 1], c, False), (one, one))
    carry = lax.cond(i % 2 == 1, lambda c: steps([i - 1, i], c, True), lambda c: steps([i], c, True), carry)
    o_t = jnp.concatenate([acc[:MLA_V] / acc[MLA_V:MLA_V + 1] for _, acc in carry], axis=0)
    o_ref[...] = o_t.T.astype(BF16)


def _mla_attention(qa, ka, vat, batch, seq):
    blk = MLA_BLOCK
    nq = seq // blk
    pairs = MLA_HEADS // 2
    return pl.pallas_call(
        _mla_kernel,
        grid=(batch, pairs, nq),
        in_specs=[pl.BlockSpec((blk, 2 * HEAD_PAD), lambda b, p, i: (b * nq + i, p)),
                  pl.BlockSpec((seq, 2 * HEAD_PAD), lambda b, p, i: (b, p)),
                  pl.BlockSpec((nq, 2 * MLA_V, blk), lambda b, p, i: (b, p, 0))],
        out_specs=pl.BlockSpec((blk, 2 * MLA_V), lambda b, p, i: (b * nq + i, p)),
        out_shape=jax.ShapeDtypeStruct((batch * seq, MLA_HEADS * MLA_V), BF16),
        compiler_params=_params("parallel", "parallel", "arbitrary"),
        name="mla_attention",
    )(qa, ka, vat)


def _rel_bias_kernel(ext_ref, o_ref):
    blk = o_ref.shape[1]
    base = jnp.broadcast_to(ext_ref[0], (blk, REL_EXT))
    tile = pltpu.roll(base, REL_EXT - blk, 1, stride=1, stride_axis=0)[:, :3 * blk]
    qchunk = lax.broadcasted_iota(jnp.int32, (blk, 3 * blk), 0) // CHUNK + REL_BACK_CHUNKS
    kchunk = lax.broadcasted_iota(jnp.int32, (blk, 3 * blk), 1) // CHUNK
    visible = (kchunk >= qchunk - REL_BACK_CHUNKS) & (kchunk <= qchunk)
    o_ref[0] = jnp.where(visible, tile, NEG_INF)


def _rel_bias_tile(rel_ext):
    blk = ATT_BLOCK
    return pl.pallas_call(
        _rel_bias_kernel,
        grid=(REL_HEADS,),
        in_specs=[pl.BlockSpec((1, 1, REL_EXT), lambda h: (h, 0, 0))],
        out_specs=pl.BlockSpec((1, blk, 3 * blk), lambda h: (h, 0, 0)),
        out_shape=jax.ShapeDtypeStruct((REL_HEADS, blk, 3 * blk), F32),
        compiler_params=_params("parallel"),
        name="rel_bias_tile",
    )(rel_ext)


def _rel_kernel(q_ref, k0_ref, k1_ref, k2_ref, v0_ref, v1_ref, v2_ref, bias_ref, o_ref):
    i = pl.program_id(1)
    blk = q_ref.shape[0]
    kwin = jnp.concatenate([k0_ref[...], k1_ref[...], k2_ref[...]], axis=0)
    vwin = jnp.concatenate([v0_ref[...], v1_ref[...], v2_ref[...]], axis=0)
    col = lax.broadcasted_iota(jnp.int32, (1, 3 * blk), 1)
    in_seq = col >= (2 - i) * blk
    left = lax.broadcasted_iota(jnp.int32, (1, LANES), 1) < REL_DIM
    heads = [(p, r) for p in range(REL_HEADS // 2) for r in range(2)]
    scores = []
    for p, r in heads:
        q = q_ref[:, p * LANES:(p + 1) * LANES]
        qm = jnp.where(left if r == 0 else jnp.logical_not(left), q, jnp.zeros_like(q))
        scores.append(lax.dot_general(qm, kwin[:, p * LANES:(p + 1) * LANES], NT_DIMS, preferred_element_type=F32))
    outs = []
    for n, (p, r) in enumerate(heads):
        s = jnp.where(in_seq, scores[n] + bias_ref[n], NEG_INF)
        m = jnp.max(s, axis=-1, keepdims=True)
        e = jnp.exp(s - m)
        l = jnp.sum(e, axis=-1, keepdims=True)
        outs.append(jnp.dot(e.astype(BF16), vwin[:, p * LANES:(p + 1) * LANES], preferred_element_type=F32) / l)
    for p in range(REL_HEADS // 2):
        o_ref[:, p * LANES:(p + 1) * LANES] = jnp.where(left, outs[2 * p], outs[2 * p + 1]).astype(BF16)


def _rel_attention(qb, kb, vb, bias, batch, seq):
    blk = ATT_BLOCK
    nq = seq // blk
    width = REL_HEADS * REL_DIM
    back = lambda d: (lambda b, i: (b * nq + jnp.maximum(i - d, 0), 0))
    tile = lambda d: pl.BlockSpec((blk, width), back(d))
    return pl.pallas_call(
        _rel_kernel,
        grid=(batch, nq),
        in_specs=[tile(0), tile(2), tile(1), tile(0), tile(2), tile(1), tile(0),
                  pl.BlockSpec((REL_HEADS, blk, 3 * blk), lambda b, i: (0, 0, 0))],
        out_specs=tile(0),
        out_shape=jax.ShapeDtypeStruct((batch * seq, width), BF16),
        compiler_params=_params("parallel", "arbitrary"),
        name="rel_attention",
    )(qb, kb, kb, kb, vb, vb, vb, bias)


def _swa_kernel(sink_ref, q_ref, kp_ref, kc_ref, vp_ref, vc_ref, o_ref):
    i = pl.program_id(1)
    blk = q_ref.shape[0]
    back = SWA_BACK_CHUNKS * CHUNK
    kwin = jnp.concatenate([kp_ref[blk - back:, :], kc_ref[...]], axis=0)
    vwin = jnp.concatenate([vp_ref[blk - back:, :], vc_ref[...]], axis=0)
    qchunk = lax.broadcasted_iota(jnp.int32, (blk, blk + back), 0) // CHUNK
    col = lax.broadcasted_iota(jnp.int32, (blk, blk + back), 1)
    kchunk = col // CHUNK
    visible = (kchunk >= qchunk) & (kchunk <= qchunk + SWA_BACK_CHUNKS) & ((col >= back) | (i > 0))
    left = lax.broadcasted_iota(jnp.int32, (1, LANES), 1) < SWA_DIM
    heads = [(t, r) for t in range(2) for r in range(2)]
    scores = []
    for t, r in heads:
        q = q_ref[:, t * LANES:(t + 1) * LANES]
        qm = jnp.where(left if r == 0 else jnp.logical_not(left), q, jnp.zeros_like(q))
        scores.append(lax.dot_general(qm, kwin, NT_DIMS, preferred_element_type=F32))
    outs = []
    for n, (t, r) in enumerate(heads):
        s = jnp.where(visible, scores[n], NEG_INF)
        sink = sink_ref[2 * r + t]
        m = jnp.maximum(jnp.max(s, axis=-1, keepdims=True), sink)
        e = jnp.exp(s - m)
        l = jnp.sum(e, axis=-1, keepdims=True) + jnp.exp(sink - m)
        outs.append(jnp.dot(e.astype(BF16), vwin, preferred_element_type=F32) / l)
    for t in range(2):
        o_ref[:, t * LANES:(t + 1) * LANES] = jnp.where(left, outs[2 * t], outs[2 * t + 1]).astype(BF16)


def _swa_attention(qc, kc, vc, sinks, batch, seq):
    blk = ATT_BLOCK
    nq = seq // blk
    cur = lambda b, i: (b * nq + i, 0)
    prev = lambda b, i: (b * nq + jnp.maximum(i - 1, 0), 0)
    width = SWA_Q_HEADS * SWA_DIM
    return pl.pallas_call(
        _swa_kernel,
        grid=(batch, nq),
        in_specs=[pl.BlockSpec(memory_space=pltpu.SMEM),
                  pl.BlockSpec((blk, width), cur),
                  pl.BlockSpec((blk, LANES), prev), pl.BlockSpec((blk, LANES), cur),
                  pl.BlockSpec((blk, LANES), prev), pl.BlockSpec((blk, LANES), cur)],
        out_specs=pl.BlockSpec((blk, width), cur),
        out_shape=jax.ShapeDtypeStruct((batch * seq, width), BF16),
        compiler_params=_params("parallel", "arbitrary"),
        name="swa_attention",
    )(sinks, qc, kc, kc, vc, vc)


def _oddeven_merge_sort_pairs(n):
    pairs = []

    def merge(lo, hi, r):
        step = r * 2
        if step < hi - lo:
            merge(lo, hi, step)
            merge(lo + r, hi, step)
            pairs.extend((i, i + r) for i in range(lo + r, hi - r, step))
        else:
            pairs.append((lo, lo + r))

    def sort(lo, hi):
        if hi - lo >= 1:
            mid = lo + (hi - lo) // 2
            sort(lo, mid)
            sort(mid + 1, hi)
            merge(lo, hi, 1)

    sort(0, n - 1)
    return pairs


def _bitonic_merge_pairs(n):
    pairs = []
    s = n // 2
    while s >= 1:
        pairs.extend((i, i + s) for i in range(n) if (i // s) % 2 == 0)
        s //= 2
    return pairs


SORT16 = _oddeven_merge_sort_pairs(PEER_TOPK)
BITONIC16 = _bitonic_merge_pairs(PEER_TOPK)
STAIRCASE = [(a, b) for a in range(1, PEER_TOPK) for b in range(PEER_TOPK) if (a + 1) * (b + 1) <= PEER_TOPK]


def _compare_exchange(vals, pairs):
    vals = list(vals)
    for a, b in pairs:
        hi = jnp.maximum(vals[a], vals[b])
        lo = jnp.minimum(vals[a], vals[b])
        vals[a], vals[b] = hi, lo
    return vals


def _merge_top16(xs, ys):
    return _compare_exchange([jnp.maximum(xs[i], ys[PEER_TOPK - 1 - i]) for i in range(PEER_TOPK)], BITONIC16)


def _sorted_top16(s):
    slabs = _compare_exchange([s[SUBLANES * v:SUBLANES * (v + 1), :] for v in range(N_KEYS // SUBLANES)], SORT16)
    for shift in (4, 2, 1):
        slabs = _merge_top16(slabs, [pltpu.roll(x, shift, 0) for x in slabs])
    return slabs


def _count_prefix(test, vals):
    cands = list(vals[:PEER_TOPK - 1])
    count = None
    step = PEER_TOPK // 2
    while step >= 1:
        c = test(cands[step - 1])
        inc = jnp.where(c, float(step), 0.0)
        count = inc if count is None else count + inc
        cands = [jnp.where(c, cands[k + step], cands[k]) for k in range(step - 1)]
        step //= 2
    return count + jnp.where(test(vals[PEER_TOPK - 1]), 1.0, 0.0)


def _take_top(s, rows, n_rows, on_pick):
    m = jnp.max(s, axis=0, keepdims=True)
    idx = jnp.min(jnp.where(s == m, rows, float(n_rows)), axis=0, keepdims=True)
    hit = rows == idx
    on_pick(m, hit)
    return jnp.where(hit, -jnp.inf, s), hit


def _route_kernel(ya_ref, yb_ref, yc_ref, x_ref, wout_ref, g_ref, b_ref, wq_ref, keys_ref,
                  x1_ref, x1b_ref, rb_ref, e2_ref, nb_ref, e1_ref,
                  q_scr, sc_scr, rank_scr, top_scr, cand_scr, w_scr):
    tt = x_ref.shape[0]
    n_a, n_b = ya_ref.shape[1], yb_ref.shape[1]
    mix = (jnp.dot(ya_ref[...], wout_ref[0:n_a, :], preferred_element_type=F32)
           + jnp.dot(yb_ref[...], wout_ref[n_a:n_a + n_b, :], preferred_element_type=F32)
           + jnp.dot(yc_ref[...], wout_ref[n_a + n_b:, :], preferred_element_type=F32))
    x1 = _layer_norm(DEEPNORM_ALPHA * x_ref[...] + mix, g_ref[...], b_ref[...])
    x1_ref[...] = x1
    x1b = x1.astype(BF16)
    x1b_ref[...] = x1b
    q = jnp.dot(x1b, wq_ref[...], preferred_element_type=F32).astype(BF16)
    for h in range(PEER_HEADS):
        q_scr[h] = q[:, h * LANES:(h + 1) * LANES]

    def emit(h, rank2, n_sel, e1, e2):
        rb_ref[h] = rank2.astype(BF16)
        e2_ref[h] = e2.astype(BF16)
        nb_ref[h] = n_sel
        e1_ref[h] = e1

    n_slab = N_KEYS // SUBLANES

    def fast_head(h, bad):
        s1 = lax.dot_general(keys_ref[2 * h], q_scr[h], NT_DIMS, preferred_element_type=F32)
        s2 = lax.dot_general(keys_ref[2 * h + 1], q_scr[h], NT_DIMS, preferred_element_type=F32)
        sc_scr[2 * h] = s1
        sc_scr[2 * h + 1] = s2
        t1 = _sorted_top16(s1)
        t2 = _sorted_top16(s2)
        n_tiles = tt // LANES
        sub = lax.broadcasted_iota(jnp.int32, (SUBLANES, LANES), 0)

        def compact(x):
            out = x[:, 0:LANES]
            for g in range(1, n_tiles):
                out = jnp.where(sub == g, x[:, g * LANES:(g + 1) * LANES], out)
            return out

        def expand(x):
            return jnp.concatenate([jnp.broadcast_to(x[g:g + 1, :], (SUBLANES, LANES)) for g in range(n_tiles)], axis=1)

        c1 = [compact(x) for x in t1]
        c2 = [compact(x) for x in t2]
        neg = jnp.full_like(c1[0], -jnp.inf)
        rest = [c1[a] + c2[b] for a, b in STAIRCASE]
        rest += [neg] * (3 * PEER_TOPK - len(rest))
        groups = [_compare_exchange(rest[g * PEER_TOPK:(g + 1) * PEER_TOPK], SORT16) for g in range(3)]
        row0 = [c1[0] + c2[b] for b in range(PEER_TOPK)]
        ctop = _merge_top16(_merge_top16(row0, groups[0]), _merge_top16(groups[1], groups[2]))
        z = jnp.exp(ctop[0] - ctop[0])
        for r in range(1, PEER_TOPK):
            z = z + jnp.exp(ctop[r] - ctop[0])
        tau = expand(ctop[PEER_TOPK - 1])
        inv_z = expand(1.0 / z)
        in_top = jnp.zeros_like(tau)
        sel_mass = jnp.zeros_like(tau)
        ranks, counts, e1s, e2s = [], [], [], []
        for v in range(n_slab):
            a1 = s1[SUBLANES * v:SUBLANES * (v + 1), :]
            a2 = s2[SUBLANES * v:SUBLANES * (v + 1), :]
            rank2 = _count_prefix(lambda top: top > a2, t2)
            n_sel = _count_prefix(lambda top: a1 + top >= tau, t2)
            in_top = in_top + jnp.minimum(float(PEER_TOPK) - rank2, 1.0)
            sel_mass = sel_mass + n_sel
            ranks.append(rank2)
            counts.append(n_sel)
            e1s.append(jnp.exp(a1 - t1[0]) * inv_z)
            e2s.append(jnp.exp(a2 - t2[0]))
        emit(h, jnp.concatenate(ranks, axis=0), jnp.concatenate(counts, axis=0),
             jnp.concatenate(e1s, axis=0), jnp.concatenate(e2s, axis=0))
        in_top = jnp.sum(in_top, axis=0, keepdims=True)
        sel_mass = jnp.sum(sel_mass, axis=0, keepdims=True)
        tied = (in_top != float(PEER_TOPK)) | (sel_mass != float(PEER_TOPK))
        return jnp.maximum(bad, jnp.where(tied, 1.0, 0.0))

    bad = lax.fori_loop(0, PEER_HEADS, fast_head, jnp.zeros((1, tt), F32))

    @pl.when(jnp.max(bad) > 0.0)
    def _exact_with_ties():
        key_rows = lax.broadcasted_iota(jnp.int32, (N_KEYS, tt), 0).astype(F32)

        def half_body(hp, _):
            s0 = sc_scr[hp]

            def pick(r, carry):
                s, rank = carry

                def on_pick(m, hit):
                    top_scr[hp, pl.ds(r, 1), :] = m

                s, hit = _take_top(s, key_rows, N_KEYS, on_pick)
                return s, jnp.where(hit, r.astype(F32), rank)

            _, rank = lax.fori_loop(0, PEER_TOPK, pick, (s0, jnp.full((N_KEYS, tt), float(PEER_TOPK), F32)))
            rank_scr[hp] = rank
            return 0

        lax.fori_loop(0, 2 * PEER_HEADS, half_body, 0)

        n_cand = PEER_TOPK * PEER_TOPK
        cand_rows = lax.broadcasted_iota(jnp.int32, (n_cand, tt), 0).astype(F32)

        def head_body(h, _):
            t1 = top_scr[2 * h]
            t2 = top_scr[2 * h + 1]
            e1t = jnp.exp(t1 - t1[0:1])
            e2t = jnp.exp(t2 - t2[0:1])
            for a in range(PEER_TOPK):
                cand_scr[a * PEER_TOPK:(a + 1) * PEER_TOPK, :] = t1[a:a + 1] + t2
                w_scr[a * PEER_TOPK:(a + 1) * PEER_TOPK, :] = e1t[a:a + 1] * e2t

            def pick(r, carry):
                c, sel = carry
                c, hit = _take_top(c, cand_rows, n_cand, lambda m, hit: None)
                return c, jnp.where(hit, 1.0, sel)

            _, sel = lax.fori_loop(0, PEER_TOPK, pick, (cand_scr[...], jnp.zeros((n_cand, tt), F32)))
            z = jnp.sum(sel * w_scr[...], axis=0, keepdims=True)
            rank1 = rank_scr[2 * h]
            n_sel = jnp.zeros((N_KEYS, tt), F32)
            for a in range(PEER_TOPK):
                cnt = jnp.sum(sel[a * PEER_TOPK:(a + 1) * PEER_TOPK, :], axis=0, keepdims=True)
                n_sel = jnp.where(rank1 == float(a), cnt, n_sel)
            emit(h, rank_scr[2 * h + 1], n_sel, jnp.exp(sc_scr[2 * h] - t1[0:1]) / z,
                 jnp.exp(sc_scr[2 * h + 1] - t2[0:1]))
            return 0

        lax.fori_loop(0, PEER_HEADS, head_body, 0)


def _route(ya, yb, yc, x2d, lp, g, b):
    t = x2d.shape[0]
    tt = min(ROUTE_TILE, t)
    full = lambda a: pl.BlockSpec(a.shape, lambda i: (0,) * a.ndim)
    row = lambda w: pl.BlockSpec((tt, w), lambda i: (i, 0))
    route_spec = pl.BlockSpec((PEER_HEADS, N_KEYS, tt), lambda i: (0, 0, i))
    route_shape = lambda dt: jax.ShapeDtypeStruct((PEER_HEADS, N_KEYS, t), dt)
    n_cand = PEER_TOPK * PEER_TOPK
    return pl.pallas_call(
        _route_kernel,
        grid=(t // tt,),
        in_specs=[row(ya.shape[1]), row(yb.shape[1]), row(yc.shape[1]), row(D_MODEL), full(lp['w_out']),
                  full(g), full(b), full(lp['wq']), full(lp['keys'])],
        out_specs=[row(D_MODEL), row(D_MODEL), route_spec, route_spec, route_spec, route_spec],
        out_shape=[jax.ShapeDtypeStruct((t, D_MODEL), F32), jax.ShapeDtypeStruct((t, D_MODEL), BF16),
                   route_shape(BF16), route_shape(BF16), route_shape(F32), route_shape(F32)],
        scratch_shapes=[pltpu.VMEM((PEER_HEADS, tt, LANES), BF16),
                        pltpu.VMEM((2 * PEER_HEADS, N_KEYS, tt), F32),
                        pltpu.VMEM((2 * PEER_HEADS, N_KEYS, tt), F32),
                        pltpu.VMEM((2 * PEER_HEADS, PEER_TOPK, tt), F32),
                        pltpu.VMEM((n_cand, tt), F32),
                        pltpu.VMEM((n_cand, tt), F32)],
        compiler_params=_params("parallel"),
        name="route",
    )(ya, yb, yc, x2d, lp['w_out'], g, b, lp['wq'], lp['keys'])


def _peer_kernel(x1b_ref, u_ref, vt_ref, rb_ref, e2_ref, nb_ref, e1_ref, x1_ref, g_ref, b_ref,
                 o_ref, acc_ref, act_ref):
    ei = pl.program_id(1)
    tt = x1b_ref.shape[0]
    n_sub = u_ref.shape[0] // N_KEYS

    @pl.when(ei == 0)
    def _():
        acc_ref[...] = jnp.zeros_like(acc_ref)

    def gate_rows(ii):
        i_row = ei * n_sub + ii
        gate = jnp.zeros((N_KEYS, tt), BF16)
        for h in range(PEER_HEADS):
            n_sel = jnp.broadcast_to(nb_ref[h, pl.ds(i_row, 1), :], (N_KEYS, tt)).astype(BF16)
            e1 = jnp.broadcast_to(e1_ref[h, pl.ds(i_row, 1), :], (N_KEYS, tt)).astype(BF16)
            gate = gate + jnp.where(rb_ref[h] < n_sel, e2_ref[h], jnp.zeros((), BF16)) * e1
        return gate

    per = n_sub // PEER_DOT_CHUNKS
    chunk_rows = per * N_KEYS
    gates = {}
    hids = []
    for c in range(PEER_DOT_CHUNKS):
        lhs = u_ref[c * chunk_rows:(c + 1) * chunk_rows, :]
        if c >= PEER_PACE_LAG:
            for ii in range((c - PEER_PACE_LAG) * per, (c - PEER_PACE_LAG + 1) * per):
                gates[ii] = gate_rows(ii)
            tile = gates[(c - PEER_PACE_LAG + 1) * per - 1][0:2 * SUBLANES, 0:LANES]
            zero = pltpu.bitcast((pltpu.bitcast(tile, jnp.uint32) >> 16) >> 16, tile.dtype)
            head = lhs[0:2 * SUBLANES, :] + jnp.concatenate([zero] * (D_MODEL // LANES), axis=1)
            lhs = jnp.concatenate([head, lhs[2 * SUBLANES:, :]], axis=0)
        hids.append(lax.dot_general(lhs, x1b_ref[...], NT_DIMS, preferred_element_type=F32))
    for ii in range((PEER_DOT_CHUNKS - PEER_PACE_LAG) * per, n_sub):
        gates[ii] = gate_rows(ii)
    for ii in range(n_sub):
        hid = hids[ii // per][(ii % per) * N_KEYS:(ii % per + 1) * N_KEYS, :].astype(BF16)
        gelu = (hid * 0.5) * (1.0 + lax.erf(hid * RSQRT2))
        act_ref[ii * N_KEYS:(ii + 1) * N_KEYS, :] = gelu * gates[ii]
    acc_ref[...] += jnp.dot(vt_ref[...], act_ref[...], preferred_element_type=F32)

    @pl.when(ei == pl.num_programs(1) - 1)
    def _():
        ffn = acc_ref[...].T
        o_ref[...] = _layer_norm(DEEPNORM_ALPHA * x1_ref[...] + ffn, g_ref[...], b_ref[...])


def _peer(x1, x1b, rb, e2, nb, e1, lp, g, b):
    t = x1.shape[0]
    tt = min(PEER_TOKEN_TILE, t)
    eb = PEER_EXPERT_BLOCK
    full = lambda a: pl.BlockSpec(a.shape, lambda ti, ei: (0,) * a.ndim)
    route_spec = pl.BlockSpec((PEER_HEADS, N_KEYS, tt), lambda ti, ei: (0, 0, ti))
    return pl.pallas_call(
        _peer_kernel,
        grid=(t // tt, N_EXPERTS // eb),
        in_specs=[pl.BlockSpec((tt, D_MODEL), lambda ti, ei: (ti, 0)),
                  pl.BlockSpec((eb, D_MODEL), lambda ti, ei: (ei, 0)),
                  pl.BlockSpec((D_MODEL, eb), lambda ti, ei: (0, ei)),
                  route_spec, route_spec, route_spec, route_spec,
                  pl.BlockSpec((tt, D_MODEL), lambda ti, ei: (ti, 0)), full(g), full(b)],
        out_specs=pl.BlockSpec((tt, D_MODEL), lambda ti, ei: (ti, 0)),
        out_shape=jax.ShapeDtypeStruct((t, D_MODEL), F32),
        scratch_shapes=[pltpu.VMEM((D_MODEL, tt), F32), pltpu.VMEM((eb, tt), BF16)],
        compiler_params=_params("parallel", "arbitrary"),
        name="peer_experts",
    )(x1b, lp['u'], lp['v_t'], rb, e2, nb, e1, x1, g, b)


def kernel(x, w_in, mla_q_norm, mla_w_uq, mla_kv_norm, mla_w_ukv, rel_bias, swa_sinks, w_out, ln1_g, ln1_b,
           peer_wq, peer_keys, peer_u, peer_v, ln2_g, ln2_b):
    batch, seq, d = x.shape
    assert d == D_MODEL and seq % PROJ_TILE == 0 and PROJ_TILE % MLA_BLOCK == 0 and MLA_BLOCK % ATT_BLOCK == 0
    tabs = _make_tables(seq)
    h = x.reshape(batch * seq, d)
    for l in range(w_in.shape[0]):
        lp = _prep_layer(w_in[l], mla_w_uq[l], mla_w_ukv[l], rel_bias[l], w_out[l], peer_wq[l], peer_keys[l],
                         peer_u[l], peer_v[l])
        row = lambda a: a[l].reshape(1, -1)
        qa, ka, vat, qb, kb, vb, qc, kc, vc = _projections(h, lp, row(mla_q_norm), row(mla_kv_norm), tabs, seq)
        ya = _mla_attention(qa, ka, vat, batch, seq)
        yb = _rel_attention(qb, kb, vb, _rel_bias_tile(lp['rel_ext']), batch, seq)
        yc = _swa_attention(qc, kc, vc, swa_sinks[l], batch, seq)
        x1, x1b, rb, e2, nb, e1 = _route(ya, yb, yc, h, lp, row(ln1_g), row(ln1_b))
        h = _peer(x1, x1b, rb, e2, nb, e1, lp, row(ln2_g), row(ln2_b))
    return h.reshape(batch, seq, d)
```

```python
import numpy as np
import jax
import jax.numpy as jnp
from jax import lax
from jax.experimental import pallas as pl
from jax.experimental.pallas import tpu as pltpu

F32 = jnp.float32
BF16 = jnp.bfloat16

D_MODEL = 1024
DEPTH = 2
CHUNK = 64
ROPE_THETA = 10000.0
LN_EPS = 1e-5
RMS_EPS = 1e-6
NEG_INF = -1e30
DEEPNORM_ALPHA = (2.0 * DEPTH) ** 0.25

MLA_HEADS = 8
MLA_Q_RANK = 384
MLA_KV_RANK = 256
MLA_NOPE = 64
MLA_ROPE = 32
MLA_V = 64
MLA_DK = MLA_NOPE + MLA_ROPE

REL_HEADS = 4
REL_DIM = 64
REL_BACK_CHUNKS = 8
MAX_REL_DIST = 256

SWA_Q_HEADS = 4
SWA_KV_HEADS = 2
SWA_DIM = 64
SWA_BACK_CHUNKS = 2

IN_SPLITS = [MLA_Q_RANK, MLA_KV_RANK, MLA_ROPE, 256, 256, 256, 256, 128, 128]
IN_SPLIT_POINTS = [int(c) for c in np.cumsum(IN_SPLITS)[:-1]]

PEER_HEADS = 8
N_KEYS = 128
N_EXPERTS = N_KEYS * N_KEYS
PEER_HALF = 64
PEER_TOPK = 16

LANES = 128
SUBLANES = 8
HEAD_PAD = 128
SWA_HEAD_ORDER = (0, 2, 1, 3)

C_AQ, C_AKV, C_KR, C_KRR = 0, 384, 640, 768
C_BQ, C_BK, C_BV = 896, 1152, 1408
C_CQ, C_CQR, C_CK, C_CKR, C_CV = 1664, 1920, 2176, 2304, 2432

PROJ_TILE = 512
ATT_BLOCK = 256
MLA_BLOCK = 512
MLA_SUM_ROWS = 16
ROUTE_TILE = 512
PEER_TOKEN_TILE = 512
PEER_EXPERT_BLOCK = 2048
PEER_DOT_CHUNKS = 8
PEER_PACE_LAG = 1
REL_EXT = 1024
VMEM_LIMIT = 56 * 1024 * 1024

NT_DIMS = (((1,), (1,)), ((), ()))
RSQRT2 = float(np.sqrt(0.5))


def _rot_half_cols(w, head_dim):
    k, c = w.shape
    w3 = w.reshape(k, c // head_dim, head_dim)
    half = head_dim // 2
    return jnp.concatenate([-w3[..., half:], w3[..., :half]], axis=-1).reshape(k, c)


def _place(w, lo, width=LANES):
    return jnp.pad(w, ((0, 0), (lo, width - lo - w.shape[1])))


def _rope_tables(seq, dim):
    inv = ROPE_THETA ** (-jnp.arange(0, dim, 2, dtype=F32) / dim)
    ang = jnp.arange(seq, dtype=F32)[:, None] * inv[None, :]
    return jnp.cos(ang), jnp.sin(ang)


def _make_tables(seq):
    c16, s16 = _rope_tables(seq, MLA_ROPE)
    c32, s32 = _rope_tables(seq, SWA_DIM)
    one = jnp.ones((seq, MLA_NOPE), F32)
    z64 = jnp.zeros((seq, MLA_NOPE), F32)
    z32 = jnp.zeros((seq, LANES - MLA_DK), F32)
    scale_a = MLA_DK ** -0.5
    cos_a = jnp.concatenate([one, c16, c16, z32], axis=1) * scale_a
    sin_a = jnp.concatenate([z64, s16, s16, z32], axis=1) * scale_a
    cos_k = jnp.concatenate([z64, c16, c16, z32], axis=1)
    sin_k = jnp.concatenate([z64, s16, s16, z32], axis=1)
    cos_c = jnp.concatenate([c32, c32, c32, c32], axis=1)
    sin_c = jnp.concatenate([s32, s32, s32, s32], axis=1)
    return jnp.stack([cos_a, sin_a, cos_k, sin_k, cos_c, sin_c])


def _prep_layer(w_in, w_uq, w_ukv, rel_bias, w_out, peer_wq, peer_keys, peer_u, peer_v):
    a_q, a_kv, a_kr, b_q, b_k, b_v, c_q, c_k, c_v = jnp.split(w_in, IN_SPLIT_POINTS, axis=1)
    c_q = c_q.reshape(D_MODEL, SWA_Q_HEADS, SWA_DIM)[:, SWA_HEAD_ORDER, :].reshape(D_MODEL, -1)
    att_scale = REL_DIM ** -0.5
    w_in_ext = jnp.concatenate([
        a_q, a_kv, _place(a_kr, MLA_NOPE), _place(_rot_half_cols(a_kr, MLA_ROPE), MLA_NOPE),
        b_q * att_scale, b_k, b_v,
        c_q * att_scale, _rot_half_cols(c_q, SWA_DIM) * att_scale, c_k, _rot_half_cols(c_k, SWA_DIM), c_v,
    ], axis=1).astype(BF16)

    uq = w_uq.reshape(MLA_Q_RANK, MLA_HEADS, MLA_DK)
    uq_pad = jnp.pad(uq, ((0, 0), (0, 0), (0, HEAD_PAD - MLA_DK))).reshape(MLA_Q_RANK, -1)
    uq_rope = uq[..., MLA_NOPE:]
    half = MLA_ROPE // 2
    uq_rot = jnp.concatenate([-uq_rope[..., half:], uq_rope[..., :half]], axis=-1)
    uq_rot = jnp.pad(uq_rot, ((0, 0), (0, 0), (MLA_NOPE, HEAD_PAD - MLA_DK))).reshape(MLA_Q_RANK, -1)
    w_uq_ext = jnp.concatenate([uq_pad, uq_rot], axis=1).astype(BF16)

    ukv = w_ukv.reshape(MLA_KV_RANK, MLA_HEADS, MLA_NOPE + MLA_V)
    w_uk = jnp.pad(ukv[..., :MLA_NOPE], ((0, 0), (0, 0), (0, HEAD_PAD - MLA_NOPE))).reshape(MLA_KV_RANK, -1).astype(BF16)
    w_uv_t = ukv[..., MLA_NOPE:].reshape(MLA_KV_RANK, -1).T.astype(BF16)

    m = jnp.arange(REL_EXT)
    ext_idx = jnp.clip(3 * ATT_BLOCK - m, -(CHUNK - 1), MAX_REL_DIST) + (CHUNK - 1)
    rel_ext = rel_bias[:, ext_idx].astype(F32).reshape(REL_HEADS, 1, REL_EXT)

    n_a = MLA_HEADS * MLA_V
    n_b = REL_HEADS * REL_DIM
    w_out_c = w_out[n_a + n_b:].reshape(SWA_Q_HEADS, SWA_DIM, D_MODEL)[SWA_HEAD_ORDER, :, :].reshape(-1, D_MODEL)
    w_out_ext = jnp.concatenate([w_out[:n_a + n_b], w_out_c], axis=0).astype(BF16)

    keys = peer_keys.reshape(PEER_HEADS * 2, N_KEYS, PEER_HALF)
    keys_lo = jnp.pad(keys, ((0, 0), (0, 0), (0, PEER_HALF)))
    keys_hi = jnp.pad(keys, ((0, 0), (0, 0), (PEER_HALF, 0)))
    is_hi = (jnp.arange(PEER_HEADS * 2) % 2 == 1)[:, None, None]
    keys_ext = jnp.where(is_hi, keys_hi, keys_lo).astype(BF16)

    return dict(w_in=w_in_ext, w_uq=w_uq_ext, w_uk=w_uk, w_uv_t=w_uv_t, rel_ext=rel_ext, w_out=w_out_ext,
                wq=peer_wq.astype(BF16), keys=keys_ext, u=peer_u.astype(BF16), v_t=peer_v.astype(BF16).T)


def _params(*sem):
    return pltpu.CompilerParams(dimension_semantics=sem, vmem_limit_bytes=VMEM_LIMIT)


def _layer_norm(z, g, b):
    mu = jnp.mean(z, axis=-1, keepdims=True)
    zc = z - mu
    var = jnp.mean(zc * zc, axis=-1, keepdims=True)
    return zc * lax.rsqrt(var + LN_EPS) * g + b


def _rms_norm(x, g):
    return x * lax.rsqrt(jnp.mean(x * x, axis=-1, keepdims=True) + RMS_EPS) * g


def _proj_kernel(x_ref, win_ref, qn_ref, wuq_ref, kvn_ref, wuk_ref, wuvt_ref, tab_ref,
                 qa_ref, ka_ref, vat_ref, qb_ref, kb_ref, vb_ref, qc_ref, kc_ref, vc_ref):
    xb = x_ref[...].astype(BF16)
    h_all = jnp.dot(xb, win_ref[...], preferred_element_type=F32)

    def proj(lo, width):
        return h_all[:, lo:lo + width]

    cos_a, sin_a, cos_k, sin_k, cos_c, sin_c = (tab_ref[i] for i in range(6))

    cqn = _rms_norm(proj(C_AQ, MLA_Q_RANK), qn_ref[...]).astype(BF16)
    rot0 = MLA_HEADS * HEAD_PAD
    q_all = jnp.dot(cqn, wuq_ref[...], preferred_element_type=F32)
    for h in range(MLA_HEADS):
        lo = h * HEAD_PAD
        qh = q_all[:, lo:lo + HEAD_PAD]
        qr = q_all[:, rot0 + lo:rot0 + lo + HEAD_PAD]
        qa_ref[:, lo:lo + HEAD_PAD] = (qh * cos_a + qr * sin_a).astype(BF16)

    ckvn = _rms_norm(proj(C_AKV, MLA_KV_RANK), kvn_ref[...]).astype(BF16)
    k_pe = proj(C_KR, LANES) * cos_k + proj(C_KRR, LANES) * sin_k
    k_all = jnp.dot(ckvn, wuk_ref[...], preferred_element_type=F32)
    for h in range(MLA_HEADS):
        lo = h * HEAD_PAD
        ka_ref[:, lo:lo + HEAD_PAD] = (k_all[:, lo:lo + HEAD_PAD] + k_pe).astype(BF16)
    for c in range(vat_ref.shape[0]):
        rows = ckvn[c * MLA_BLOCK:(c + 1) * MLA_BLOCK, :]
        vat_ref[c] = lax.dot_general(wuvt_ref[...], rows, NT_DIMS, preferred_element_type=F32).astype(BF16)

    qb_ref[...] = proj(C_BQ, 256).astype(BF16)
    kb_ref[...] = proj(C_BK, 256).astype(BF16)
    vb_ref[...] = proj(C_BV, 256).astype(BF16)

    for t in range(2):
        lo = t * LANES
        qc_ref[:, lo:lo + LANES] = (proj(C_CQ + lo, LANES) * cos_c + proj(C_CQR + lo, LANES) * sin_c).astype(BF16)
    kc_ref[...] = (proj(C_CK, LANES) * cos_c + proj(C_CKR, LANES) * sin_c).astype(BF16)
    vc_ref[...] = proj(C_CV, LANES).astype(BF16)


def _projections(x2d, lp, qn, kvn, tabs, seq):
    t = x2d.shape[0]
    tile = min(PROJ_TILE, seq)
    n_seq = seq // tile
    sub = tile // MLA_BLOCK
    full = lambda a: pl.BlockSpec(a.shape, lambda i: (0,) * a.ndim)
    row = lambda w: pl.BlockSpec((tile, w), lambda i: (i, 0))
    sds = lambda w: jax.ShapeDtypeStruct((t, w), BF16)
    n_v = MLA_HEADS * MLA_V
    vat_spec = pl.BlockSpec((sub, n_v, MLA_BLOCK), lambda i: (i, 0, 0))
    vat_sds = jax.ShapeDtypeStruct((t // MLA_BLOCK, n_v, MLA_BLOCK), BF16)
    widths = (256, 256, 256, 256, 128, 128)
    return pl.pallas_call(
        _proj_kernel,
        grid=(t // tile,),
        in_specs=[row(D_MODEL), full(lp['w_in']), full(qn), full(lp['w_uq']), full(kvn), full(lp['w_uk']),
                  full(lp['w_uv_t']), pl.BlockSpec((6, tile, LANES), lambda i: (0, i % n_seq, 0))],
        out_specs=[row(1024), row(1024), vat_spec] + [row(w) for w in widths],
        out_shape=[sds(1024), sds(1024), vat_sds] + [sds(w) for w in widths],
        compiler_params=_params("parallel"),
        name="projections",
    )(x2d, lp['w_in'], qn, lp['w_uq'], kvn, lp['w_uk'], lp['w_uv_t'], tabs)


def _mla_kernel(q_ref, k_ref, vt_ref, o_ref):
    blk = vt_ref.shape[2]
    kchunk = lax.broadcasted_iota(jnp.int32, (blk, blk), 0) // CHUNK
    qchunk = lax.broadcasted_iota(jnp.int32, (blk, blk), 1) // CHUNK
    diag_visible = kchunk <= qchunk
    ones_rows = (lax.broadcasted_iota(jnp.int32, (MLA_SUM_ROWS, blk), 0) == 0).astype(BF16)
    one = (jnp.full((1, blk), NEG_INF, F32), jnp.zeros((MLA_V + MLA_SUM_ROWS, blk), F32))

    def query_block(i, _):
        rows = pl.ds(pl.multiple_of(i * blk, blk), blk)
        qs = [q_ref[rows, r * HEAD_PAD:(r + 1) * HEAD_PAD] for r in range(2)]

        def steps(blocks, carry, last_is_diagonal):
            scores = [[lax.dot_general(k_ref[pl.ds(pl.multiple_of(j * blk, blk), blk), r * HEAD_PAD:(r + 1) * HEAD_PAD],
                                       qs[r], NT_DIMS, preferred_element_type=F32) for r in range(2)]
                      for j in blocks]
            for n, j in enumerate(blocks):
                out = []
                for r in range(2):
                    m, acc = carry[r]
                    s = scores[n][r]
                    if last_is_diagonal and n == len(blocks) - 1:
                        s = jnp.where(diag_visible, s, NEG_INF)
                    m_new = jnp.maximum(m, jnp.max(s, axis=0, keepdims=True))
                    p = jnp.exp((s - m_new).astype(BF16))
                    a = jnp.exp(m - m_new)
                    v_aug = jnp.concatenate([vt_ref[j, r * MLA_V:(r + 1) * MLA_V, :], ones_rows], axis=0)
                    out.append((m_new, a * acc + jnp.dot(v_aug, p, preferred_element_type=F32)))
                carry = tuple(out)
            return carry

        carry = lax.fori_loop(0, i // 2, lambda t, c: steps([2 * t, 2 * t + 1], c, False), (one, one))
        carry = lax.cond(i % 2 == 1, lambda c: steps([i - 1, i], c, True), lambda c: steps([i], c, True), carry)
        o_t = jnp.concatenate([acc[:MLA_V] / acc[MLA_V:MLA_V + 1] for _, acc in carry], axis=0)
        o_ref[rows, :] = o_t.T.astype(BF16)
        return 0

    lax.fori_loop(0, vt_ref.shape[0], query_block, 0)


def _mla_attention(qa, ka, vat, batch, seq):
    blk = MLA_BLOCK
    nq = seq // blk
    pairs = MLA_HEADS // 2
    return pl.pallas_call(
        _mla_kernel,
        grid=(batch, pairs),
        in_specs=[pl.BlockSpec((seq, 2 * HEAD_PAD), lambda b, p: (b, p)),
                  pl.BlockSpec((seq, 2 * HEAD_PAD), lambda b, p: (b, p)),
                  pl.BlockSpec((nq, 2 * MLA_V, blk), lambda b, p: (b, p, 0))],
        out_specs=pl.BlockSpec((seq, 2 * MLA_V), lambda b, p: (b, p)),
        out_shape=jax.ShapeDtypeStruct((batch * seq, MLA_HEADS * MLA_V), BF16),
        compiler_params=_params("parallel", "parallel"),
        name="mla_attention",
    )(qa, ka, vat)


def _rel_bias_kernel(ext_ref, o_ref):
    blk = o_ref.shape[1]
    base = jnp.broadcast_to(ext_ref[0], (blk, REL_EXT))
    tile = pltpu.roll(base, REL_EXT - blk, 1, stride=1, stride_axis=0)[:, :3 * blk]
    qchunk = lax.broadcasted_iota(jnp.int32, (blk, 3 * blk), 0) // CHUNK + REL_BACK_CHUNKS
    kchunk = lax.broadcasted_iota(jnp.int32, (blk, 3 * blk), 1) // CHUNK
    visible = (kchunk >= qchunk - REL_BACK_CHUNKS) & (kchunk <= qchunk)
    o_ref[0] = jnp.where(visible, tile, NEG_INF)


def _rel_bias_tile(rel_ext):
    blk = ATT_BLOCK
    return pl.pallas_call(
        _rel_bias_kernel,
        grid=(REL_HEADS,),
        in_specs=[pl.BlockSpec((1, 1, REL_EXT), lambda h: (h, 0, 0))],
        out_specs=pl.BlockSpec((1, blk, 3 * blk), lambda h: (h, 0, 0)),
        out_shape=jax.ShapeDtypeStruct((REL_HEADS, blk, 3 * blk), F32),
        compiler_params=_params("parallel"),
        name="rel_bias_tile",
    )(rel_ext)


def _rel_kernel(q_ref, k0_ref, k1_ref, k2_ref, v0_ref, v1_ref, v2_ref, bias_ref, o_ref):
    i = pl.program_id(1)
    blk = q_ref.shape[0]
    kwin = jnp.concatenate([k0_ref[...], k1_ref[...], k2_ref[...]], axis=0)
    vwin = jnp.concatenate([v0_ref[...], v1_ref[...], v2_ref[...]], axis=0)
    col = lax.broadcasted_iota(jnp.int32, (1, 3 * blk), 1)
    in_seq = col >= (2 - i) * blk
    left = lax.broadcasted_iota(jnp.int32, (1, LANES), 1) < REL_DIM
    heads = [(p, r) for p in range(REL_HEADS // 2) for r in range(2)]
    scores = []
    for p, r in heads:
        q = q_ref[:, p * LANES:(p + 1) * LANES]
        qm = jnp.where(left if r == 0 else jnp.logical_not(left), q, jnp.zeros_like(q))
        scores.append(lax.dot_general(qm, kwin[:, p * LANES:(p + 1) * LANES], NT_DIMS, preferred_element_type=F32))
    outs = []
    for n, (p, r) in enumerate(heads):
        s = jnp.where(in_seq, scores[n] + bias_ref[n], NEG_INF)
        m = jnp.max(s, axis=-1, keepdims=True)
        e = jnp.exp(s - m)
        l = jnp.sum(e, axis=-1, keepdims=True)
        outs.append(jnp.dot(e.astype(BF16), vwin[:, p * LANES:(p + 1) * LANES], preferred_element_type=F32) / l)
    for p in range(REL_HEADS // 2):
        o_ref[:, p * LANES:(p + 1) * LANES] = jnp.where(left, outs[2 * p], outs[2 * p + 1]).astype(BF16)


def _rel_attention(qb, kb, vb, bias, batch, seq):
    blk = ATT_BLOCK
    nq = seq // blk
    width = REL_HEADS * REL_DIM
    back = lambda d: (lambda b, i: (b * nq + jnp.maximum(i - d, 0), 0))
    tile = lambda d: pl.BlockSpec((blk, width), back(d))
    return pl.pallas_call(
        _rel_kernel,
        grid=(batch, nq),
        in_specs=[tile(0), tile(2), tile(1), tile(0), tile(2), tile(1), tile(0),
                  pl.BlockSpec((REL_HEADS, blk, 3 * blk), lambda b, i: (0, 0, 0))],
        out_specs=tile(0),
        out_shape=jax.ShapeDtypeStruct((batch * seq, width), BF16),
        compiler_params=_params("parallel", "arbitrary"),
        name="rel_attention",
    )(qb, kb, kb, kb, vb, vb, vb, bias)


def _swa_kernel(sink_ref, q_ref, kp_ref, kc_ref, vp_ref, vc_ref, o_ref):
    i = pl.program_id(1)
    blk = q_ref.shape[0]
    back = SWA_BACK_CHUNKS * CHUNK
    kwin = jnp.concatenate([kp_ref[blk - back:, :], kc_ref[...]], axis=0)
    vwin = jnp.concatenate([vp_ref[blk - back:, :], vc_ref[...]], axis=0)
    qchunk = lax.broadcasted_iota(jnp.int32, (blk, blk + back), 0) // CHUNK
    col = lax.broadcasted_iota(jnp.int32, (blk, blk + back), 1)
    kchunk = col // CHUNK
    visible = (kchunk >= qchunk) & (kchunk <= qchunk + SWA_BACK_CHUNKS) & ((col >= back) | (i > 0))
    left = lax.broadcasted_iota(jnp.int32, (1, LANES), 1) < SWA_DIM
    heads = [(t, r) for t in range(2) for r in range(2)]
    scores = []
    for t, r in heads:
        q = q_ref[:, t * LANES:(t + 1) * LANES]
        qm = jnp.where(left if r == 0 else jnp.logical_not(left), q, jnp.zeros_like(q))
        scores.append(lax.dot_general(qm, kwin, NT_DIMS, preferred_element_type=F32))
    outs = []
    for n, (t, r) in enumerate(heads):
        s = jnp.where(visible, scores[n], NEG_INF)
        sink = sink_ref[2 * r + t]
        m = jnp.maximum(jnp.max(s, axis=-1, keepdims=True), sink)
        e = jnp.exp(s - m)
        l = jnp.sum(e, axis=-1, keepdims=True) + jnp.exp(sink - m)
        outs.append(jnp.dot(e.astype(BF16), vwin, preferred_element_type=F32) / l)
    for t in range(2):
        o_ref[:, t * LANES:(t + 1) * LANES] = jnp.where(left, outs[2 * t], outs[2 * t + 1]).astype(BF16)


def _swa_attention(qc, kc, vc, sinks, batch, seq):
    blk = ATT_BLOCK
    nq = seq // blk
    cur = lambda b, i: (b * nq + i, 0)
    prev = lambda b, i: (b * nq + jnp.maximum(i - 1, 0), 0)
    width = SWA_Q_HEADS * SWA_DIM
    return pl.pallas_call(
        _swa_kernel,
        grid=(batch, nq),
        in_specs=[pl.BlockSpec(memory_space=pltpu.SMEM),
                  pl.BlockSpec((blk, width), cur),
                  pl.BlockSpec((blk, LANES), prev), pl.BlockSpec((blk, LANES), cur),
                  pl.BlockSpec((blk, LANES), prev), pl.BlockSpec((blk, LANES), cur)],
        out_specs=pl.BlockSpec((blk, width), cur),
        out_shape=jax.ShapeDtypeStruct((batch * seq, width), BF16),
        compiler_params=_params("parallel", "arbitrary"),
        name="swa_attention",
    )(sinks, qc, kc, kc, vc, vc)


def _oddeven_merge_sort_pairs(n):
    pairs = []

    def merge(lo, hi, r):
        step = r * 2
        if step < hi - lo:
            merge(lo, hi, step)
            merge(lo + r, hi, step)
            pairs.extend((i, i + r) for i in range(lo + r, hi - r, step))
        else:
            pairs.append((lo, lo + r))

    def sort(lo, hi):
        if hi - lo >= 1:
            mid = lo + (hi - lo) // 2
            sort(lo, mid)
            sort(mid + 1, hi)
            merge(lo, hi, 1)

    sort(0, n - 1)
    return pairs


def _bitonic_merge_pairs(n):
    pairs = []
    s = n // 2
    while s >= 1:
        pairs.extend((i, i + s) for i in range(n) if (i // s) % 2 == 0)
        s //= 2
    return pairs


SORT16 = _oddeven_merge_sort_pairs(PEER_TOPK)
BITONIC16 = _bitonic_merge_pairs(PEER_TOPK)
STAIRCASE = [(a, b) for a in range(1, PEER_TOPK) for b in range(PEER_TOPK) if (a + 1) * (b + 1) <= PEER_TOPK]


def _compare_exchange(vals, pairs):
    vals = list(vals)
    for a, b in pairs:
        hi = jnp.maximum(vals[a], vals[b])
        lo = jnp.minimum(vals[a], vals[b])
        vals[a], vals[b] = hi, lo
    return vals


def _merge_top16(xs, ys):
    return _compare_exchange([jnp.maximum(xs[i], ys[PEER_TOPK - 1 - i]) for i in range(PEER_TOPK)], BITONIC16)


def _sorted_top16(s):
    slabs = _compare_exchange([s[SUBLANES * v:SUBLANES * (v + 1), :] for v in range(N_KEYS // SUBLANES)], SORT16)
    for shift in (4, 2, 1):
        slabs = _merge_top16(slabs, [pltpu.roll(x, shift, 0) for x in slabs])
    return slabs


def _count_prefix(test, vals):
    cands = list(vals[:PEER_TOPK - 1])
    count = None
    step = PEER_TOPK // 2
    while step >= 1:
        c = test(cands[step - 1])
        inc = jnp.where(c, float(step), 0.0)
        count = inc if count is None else count + inc
        cands = [jnp.where(c, cands[k + step], cands[k]) for k in range(step - 1)]
        step //= 2
    return count + jnp.where(test(vals[PEER_TOPK - 1]), 1.0, 0.0)


def _take_top(s, rows, n_rows, on_pick):
    m = jnp.max(s, axis=0, keepdims=True)
    idx = jnp.min(jnp.where(s == m, rows, float(n_rows)), axis=0, keepdims=True)
    hit = rows == idx
    on_pick(m, hit)
    return jnp.where(hit, -jnp.inf, s), hit


def _route_kernel(ya_ref, yb_ref, yc_ref, x_ref, wout_ref, g_ref, b_ref, wq_ref, keys_ref,
                  x1_ref, x1b_ref, rb_ref, e2_ref, nb_ref, e1_ref,
                  q_scr, sc_scr, rank_scr, top_scr, cand_scr, w_scr):
    tt = x_ref.shape[0]
    n_a, n_b = ya_ref.shape[1], yb_ref.shape[1]
    mix = (jnp.dot(ya_ref[...], wout_ref[0:n_a, :], preferred_element_type=F32)
           + jnp.dot(yb_ref[...], wout_ref[n_a:n_a + n_b, :], preferred_element_type=F32)
           + jnp.dot(yc_ref[...], wout_ref[n_a + n_b:, :], preferred_element_type=F32))
    x1 = _layer_norm(DEEPNORM_ALPHA * x_ref[...] + mix, g_ref[...], b_ref[...])
    x1_ref[...] = x1
    x1b = x1.astype(BF16)
    x1b_ref[...] = x1b
    q = jnp.dot(x1b, wq_ref[...], preferred_element_type=F32).astype(BF16)
    for h in range(PEER_HEADS):
        q_scr[h] = q[:, h * LANES:(h + 1) * LANES]

    def emit(h, rank2, n_sel, e1, e2):
        rb_ref[h] = rank2.astype(BF16)
        e2_ref[h] = e2.astype(BF16)
        nb_ref[h] = n_sel
        e1_ref[h] = e1

    n_slab = N_KEYS // SUBLANES

    def fast_head(h, bad):
        s1 = lax.dot_general(keys_ref[2 * h], q_scr[h], NT_DIMS, preferred_element_type=F32)
        s2 = lax.dot_general(keys_ref[2 * h + 1], q_scr[h], NT_DIMS, preferred_element_type=F32)
        sc_scr[2 * h] = s1
        sc_scr[2 * h + 1] = s2
        t1 = _sorted_top16(s1)
        t2 = _sorted_top16(s2)
        n_tiles = tt // LANES
        sub = lax.broadcasted_iota(jnp.int32, (SUBLANES, LANES), 0)

        def compact(x):
            out = x[:, 0:LANES]
            for g in range(1, n_tiles):
                out = jnp.where(sub == g, x[:, g * LANES:(g + 1) * LANES], out)
            return out

        def expand(x):
            return jnp.concatenate([jnp.broadcast_to(x[g:g + 1, :], (SUBLANES, LANES)) for g in range(n_tiles)], axis=1)

        c1 = [compact(x) for x in t1]
        c2 = [compact(x) for x in t2]
        neg = jnp.full_like(c1[0], -jnp.inf)
        rest = [c1[a] + c2[b] for a, b in STAIRCASE]
        rest += [neg] * (3 * PEER_TOPK - len(rest))
        groups = [_compare_exchange(rest[g * PEER_TOPK:(g + 1) * PEER_TOPK], SORT16) for g in range(3)]
        row0 = [c1[0] + c2[b] for b in range(PEER_TOPK)]
        ctop = _merge_top16(_merge_top16(row0, groups[0]), _merge_top16(groups[1], groups[2]))
        z = jnp.exp(ctop[0] - ctop[0])
        for r in range(1, PEER_TOPK):
            z = z + jnp.exp(ctop[r] - ctop[0])
        tau = expand(ctop[PEER_TOPK - 1])
        inv_z = expand(1.0 / z)
        in_top = jnp.zeros_like(tau)
        sel_mass = jnp.zeros_like(tau)
        ranks, counts, e1s, e2s = [], [], [], []
        for v in range(n_slab):
            a1 = s1[SUBLANES * v:SUBLANES * (v + 1), :]
            a2 = s2[SUBLANES * v:SUBLANES * (v + 1), :]
            rank2 = _count_prefix(lambda top: top > a2, t2)
            n_sel = _count_prefix(lambda top: a1 + top >= tau, t2)
            in_top = in_top + jnp.minimum(float(PEER_TOPK) - rank2, 1.0)
            sel_mass = sel_mass + n_sel
            ranks.append(rank2)
            counts.append(n_sel)
            e1s.append(jnp.exp(a1 - t1[0]) * inv_z)
            e2s.append(jnp.exp(a2 - t2[0]))
        emit(h, jnp.concatenate(ranks, axis=0), jnp.concatenate(counts, axis=0),
             jnp.concatenate(e1s, axis=0), jnp.concatenate(e2s, axis=0))
        in_top = jnp.sum(in_top, axis=0, keepdims=True)
        sel_mass = jnp.sum(sel_mass, axis=0, keepdims=True)
        tied = (in_top != float(PEER_TOPK)) | (sel_mass != float(PEER_TOPK))
        return jnp.maximum(bad, jnp.where(tied, 1.0, 0.0))

    bad = lax.fori_loop(0, PEER_HEADS, fast_head, jnp.zeros((1, tt), F32))

    @pl.when(jnp.max(bad) > 0.0)
    def _exact_with_ties():
        key_rows = lax.broadcasted_iota(jnp.int32, (N_KEYS, tt), 0).astype(F32)

        def half_body(hp, _):
            s0 = sc_scr[hp]

            def pick(r, carry):
                s, rank = carry

                def on_pick(m, hit):
                    top_scr[hp, pl.ds(r, 1), :] = m

                s, hit = _take_top(s, key_rows, N_KEYS, on_pick)
                return s, jnp.where(hit, r.astype(F32), rank)

            _, rank = lax.fori_loop(0, PEER_TOPK, pick, (s0, jnp.full((N_KEYS, tt), float(PEER_TOPK), F32)))
            rank_scr[hp] = rank
            return 0

        lax.fori_loop(0, 2 * PEER_HEADS, half_body, 0)

        n_cand = PEER_TOPK * PEER_TOPK
        cand_rows = lax.broadcasted_iota(jnp.int32, (n_cand, tt), 0).astype(F32)

        def head_body(h, _):
            t1 = top_scr[2 * h]
            t2 = top_scr[2 * h + 1]
            e1t = jnp.exp(t1 - t1[0:1])
            e2t = jnp.exp(t2 - t2[0:1])
            for a in range(PEER_TOPK):
                cand_scr[a * PEER_TOPK:(a + 1) * PEER_TOPK, :] = t1[a:a + 1] + t2
                w_scr[a * PEER_TOPK:(a + 1) * PEER_TOPK, :] = e1t[a:a + 1] * e2t

            def pick(r, carry):
                c, sel = carry
                c, hit = _take_top(c, cand_rows, n_cand, lambda m, hit: None)
                return c, jnp.where(hit, 1.0, sel)

            _, sel = lax.fori_loop(0, PEER_TOPK, pick, (cand_scr[...], jnp.zeros((n_cand, tt), F32)))
            z = jnp.sum(sel * w_scr[...], axis=0, keepdims=True)
            rank1 = rank_scr[2 * h]
            n_sel = jnp.zeros((N_KEYS, tt), F32)
            for a in range(PEER_TOPK):
                cnt = jnp.sum(sel[a * PEER_TOPK:(a + 1) * PEER_TOPK, :], axis=0, keepdims=True)
                n_sel = jnp.where(rank1 == float(a), cnt, n_sel)
            emit(h, rank_scr[2 * h + 1], n_sel, jnp.exp(sc_scr[2 * h] - t1[0:1]) / z,
                 jnp.exp(sc_scr[2 * h + 1] - t2[0:1]))
            return 0

        lax.fori_loop(0, PEER_HEADS, head_body, 0)


def _route(ya, yb, yc, x2d, lp, g, b):
    t = x2d.shape[0]
    tt = min(ROUTE_TILE, t)
    full = lambda a: pl.BlockSpec(a.shape, lambda i: (0,) * a.ndim)
    row = lambda w: pl.BlockSpec((tt, w), lambda i: (i, 0))
    route_spec = pl.BlockSpec((PEER_HEADS, N_KEYS, tt), lambda i: (0, 0, i))
    route_shape = lambda dt: jax.ShapeDtypeStruct((PEER_HEADS, N_KEYS, t), dt)
    n_cand = PEER_TOPK * PEER_TOPK
    return pl.pallas_call(
        _route_kernel,
        grid=(t // tt,),
        in_specs=[row(ya.shape[1]), row(yb.shape[1]), row(yc.shape[1]), row(D_MODEL), full(lp['w_out']),
                  full(g), full(b), full(lp['wq']), full(lp['keys'])],
        out_specs=[row(D_MODEL), row(D_MODEL), route_spec, route_spec, route_spec, route_spec],
        out_shape=[jax.ShapeDtypeStruct((t, D_MODEL), F32), jax.ShapeDtypeStruct((t, D_MODEL), BF16),
                   route_shape(BF16), route_shape(BF16), route_shape(F32), route_shape(F32)],
        scratch_shapes=[pltpu.VMEM((PEER_HEADS, tt, LANES), BF16),
                        pltpu.VMEM((2 * PEER_HEADS, N_KEYS, tt), F32),
                        pltpu.VMEM((2 * PEER_HEADS, N_KEYS, tt), F32),
                        pltpu.VMEM((2 * PEER_HEADS, PEER_TOPK, tt), F32),
                        pltpu.VMEM((n_cand, tt), F32),
                        pltpu.VMEM((n_cand, tt), F32)],
        compiler_params=_params("parallel"),
        name="route",
    )(ya, yb, yc, x2d, lp['w_out'], g, b, lp['wq'], lp['keys'])


def _peer_kernel(x1b_ref, u_ref, vt_ref, rb_ref, e2_ref, nb_ref, e1_ref, x1_ref, g_ref, b_ref,
                 o_ref, acc_ref, act_ref):
    ei = pl.program_id(1)
    tt = x1b_ref.shape[0]
    n_sub = u_ref.shape[0] // N_KEYS

    @pl.when(ei == 0)
    def _():
        acc_ref[...] = jnp.zeros_like(acc_ref)

    def gate_rows(ii):
        i_row = ei * n_sub + ii
        gate = jnp.zeros((N_KEYS, tt), BF16)
        for h in range(PEER_HEADS):
            n_sel = jnp.broadcast_to(nb_ref[h, pl.ds(i_row, 1), :], (N_KEYS, tt)).astype(BF16)
            e1 = jnp.broadcast_to(e1_ref[h, pl.ds(i_row, 1), :], (N_KEYS, tt)).astype(BF16)
            gate = gate + jnp.where(rb_ref[h] < n_sel, e2_ref[h], jnp.zeros((), BF16)) * e1
        return gate

    per = n_sub // PEER_DOT_CHUNKS
    chunk_rows = per * N_KEYS
    gates = {}
    hids = []
    for c in range(PEER_DOT_CHUNKS):
        lhs = u_ref[c * chunk_rows:(c + 1) * chunk_rows, :]
        if c >= PEER_PACE_LAG:
            for ii in range((c - PEER_PACE_LAG) * per, (c - PEER_PACE_LAG + 1) * per):
                gates[ii] = gate_rows(ii)
            tile = gates[(c - PEER_PACE_LAG + 1) * per - 1][0:2 * SUBLANES, 0:LANES]
            zero = pltpu.bitcast((pltpu.bitcast(tile, jnp.uint32) >> 16) >> 16, tile.dtype)
            head = lhs[0:2 * SUBLANES, :] + jnp.concatenate([zero] * (D_MODEL // LANES), axis=1)
            lhs = jnp.concatenate([head, lhs[2 * SUBLANES:, :]], axis=0)
        hids.append(lax.dot_general(lhs, x1b_ref[...], NT_DIMS, preferred_element_type=F32))
    for ii in range((PEER_DOT_CHUNKS - PEER_PACE_LAG) * per, n_sub):
        gates[ii] = gate_rows(ii)
    for ii in range(n_sub):
        hid = hids[ii // per][(ii % per) * N_KEYS:(ii % per + 1) * N_KEYS, :].astype(BF16)
        gelu = (hid * 0.5) * (1.0 + lax.erf(hid * RSQRT2))
        act_ref[ii * N_KEYS:(ii + 1) * N_KEYS, :] = gelu * gates[ii]
    acc_ref[...] += jnp.dot(vt_ref[...], act_ref[...], preferred_element_type=F32)

    @pl.when(ei == pl.num_programs(1) - 1)
    def _():
        ffn = acc_ref[...].T
        o_ref[...] = _layer_norm(DEEPNORM_ALPHA * x1_ref[...] + ffn, g_ref[...], b_ref[...])


def _peer(x1, x1b, rb, e2, nb, e1, lp, g, b):
    t = x1.shape[0]
    tt = min(PEER_TOKEN_TILE, t)
    eb = PEER_EXPERT_BLOCK
    full = lambda a: pl.BlockSpec(a.shape, lambda ti, ei: (0,) * a.ndim)
    route_spec = pl.BlockSpec((PEER_HEADS, N_KEYS, tt), lambda ti, ei: (0, 0, ti))
    return pl.pallas_call(
        _peer_kernel,
        grid=(t // tt, N_EXPERTS // eb),
        in_specs=[pl.BlockSpec((tt, D_MODEL), lambda ti, ei: (ti, 0)),
                  pl.BlockSpec((eb, D_MODEL), lambda ti, ei: (ei, 0)),
                  pl.BlockSpec((D_MODEL, eb), lambda ti, ei: (0, ei)),
                  route_spec, route_spec, route_spec, route_spec,
                  pl.BlockSpec((tt, D_MODEL), lambda ti, ei: (ti, 0)), full(g), full(b)],
        out_specs=pl.BlockSpec((tt, D_MODEL), lambda ti, ei: (ti, 0)),
        out_shape=jax.ShapeDtypeStruct((t, D_MODEL), F32),
        scratch_shapes=[pltpu.VMEM((D_MODEL, tt), F32), pltpu.VMEM((eb, tt), BF16)],
        compiler_params=_params("parallel", "arbitrary"),
        name="peer_experts",
    )(x1b, lp['u'], lp['v_t'], rb, e2, nb, e1, x1, g, b)


def kernel(x, w_in, mla_q_norm, mla_w_uq, mla_kv_norm, mla_w_ukv, rel_bias, swa_sinks, w_out, ln1_g, ln1_b,
           peer_wq, peer_keys, peer_u, peer_v, ln2_g, ln2_b):
    batch, seq, d = x.shape
    assert d == D_MODEL and seq % PROJ_TILE == 0 and PROJ_TILE % MLA_BLOCK == 0 and MLA_BLOCK % ATT_BLOCK == 0
    tabs = _make_tables(seq)
    h = x.reshape(batch * seq, d)
    for l in range(w_in.shape[0]):
        lp = _prep_layer(w_in[l], mla_w_uq[l], mla_w_ukv[l], rel_bias[l], w_out[l], peer_wq[l], peer_keys[l],
                         peer_u[l], peer_v[l])
        row = lambda a: a[l].reshape(1, -1)
        qa, ka, vat, qb, kb, vb, qc, kc, vc = _projections(h, lp, row(mla_q_norm), row(mla_kv_norm), tabs, seq)
        ya = _mla_attention(qa, ka, vat, batch, seq)
        yb = _rel_attention(qb, kb, vb, _rel_bias_tile(lp['rel_ext']), batch, seq)
        yc = _swa_attention(qc, kc, vc, swa_sinks[l], batch, seq)
        x1, x1b, rb, e2, nb, e1 = _route(ya, yb, yc, h, lp, row(ln1_g), row(ln1_b))
        h = _peer(x1, x1b, rb, e2, nb, e1, lp, row(ln2_g), row(ln2_b))
    return h.reshape(batch, seq, d)
```
